```python
import math
import jax, jax.numpy as jnp
from jax import lax
import numpy as np

D_MODEL = 4096
BATCH = 4
SEQ = 2048
DEPTH = 1

GRID_W = 64
CTX_LEN = 256
HEAD_DIM = 128
N_Q_HEADS = 16
N_KV_HEADS = 4
Q_PER_KV = N_Q_HEADS // N_KV_HEADS
ATTN_WIDTH = N_Q_HEADS * HEAD_DIM
KV_WIDTH = N_KV_HEADS * HEAD_DIM
Q_BLOCK = 128
ROPE_THETA = 10000.0
AXIS_DIM = HEAD_DIM // 2
HYENA_WIDTH = D_MODEL // 2
HYENA_ORDER = 2
SHORT_CONV = 3
FILTER_EMB = 33
FILTER_HIDDEN = 64
FILTER_DIRS = 2
FILTER_INIT_SCALE = 0.1
DECAY_TARGET = 1e-2
FAST_DECAY_PCT = 0.3
SLOW_DECAY_PCT = 1.5
D_FF = 11008
MACARON_W = 0.5
N_MOD = 9
NORM_EPS = 1e-6

Q_END = ATTN_WIDTH
K_END = Q_END + KV_WIDTH
V_END = K_END + KV_WIDTH
HY_END = V_END + (HYENA_ORDER + 1) * HYENA_WIDTH
IN_COLS = HY_END + 2 * D_MODEL

kernel_name = "hybrid_gqa_hyena_macaron_dit_layer"


def rms_norm(x, g):
    x32 = x.astype(jnp.float32)
    y = x32 * lax.rsqrt(jnp.mean(x32 * x32, axis=-1, keepdims=True) + NORM_EPS)
    return (y * g.astype(jnp.float32)).astype(x.dtype)


def modulate(x, g, shift, scale):
    return rms_norm(x, g) * (1.0 + scale) + shift


def grid_positions(n_tokens):
    rows = n_tokens // GRID_W
    row = jnp.repeat(jnp.arange(rows, dtype=jnp.int32), GRID_W)
    col = jnp.tile(jnp.arange(GRID_W, dtype=jnp.int32), rows)
    return row, col


def _rotate(xa, ang):
    x1, x2 = jnp.split(xa, 2, axis=-1)
    cos = jnp.cos(ang)[None, :, None, :]
    sin = jnp.sin(ang)[None, :, None, :]
    return jnp.concatenate([x1 * cos - x2 * sin, x1 * sin + x2 * cos], axis=-1)


def rope_2d(x, row, col):
    inv_freq = ROPE_THETA ** (-jnp.arange(0, AXIS_DIM, 2, dtype=jnp.float32) / AXIS_DIM)
    x32 = x.astype(jnp.float32)
    xr = _rotate(x32[..., :AXIS_DIM], row.astype(jnp.float32)[:, None] * inv_freq)
    xc = _rotate(x32[..., AXIS_DIM:], col.astype(jnp.float32)[:, None] * inv_freq)
    return jnp.concatenate([xr, xc], axis=-1).astype(x.dtype)


def split_heads(p, n_heads):
    return p.reshape(p.shape[0], p.shape[1], n_heads, HEAD_DIM)


def attend(q, k, v):
    s = jnp.einsum('bqkgd,btkd->bkgqt', q, k, preferred_element_type=jnp.float32) / math.sqrt(HEAD_DIM)
    p = jax.nn.softmax(s, axis=-1).astype(v.dtype)
    o = jnp.einsum('bkgqt,btkd->bqkgd', p, v)
    return o.reshape(o.shape[0], o.shape[1], ATTN_WIDTH)


def latent_attention(q, k_all, v_all):
    b, s = q.shape[0], q.shape[1]
    nb = s // Q_BLOCK
    qb = q.reshape(b, nb, Q_BLOCK, N_KV_HEADS, Q_PER_KV, HEAD_DIM).swapaxes(0, 1)
    o = lax.map(lambda blk: attend(blk, k_all, v_all), qb)
    return o.swapaxes(0, 1).reshape(b, s, ATTN_WIDTH)


def kv_heads(p_kv, k_norm_g):
    p_k, p_v = jnp.split(p_kv, 2, axis=-1)
    return rms_norm(split_heads(p_k, N_KV_HEADS), k_norm_g), split_heads(p_v, N_KV_HEADS)


def hyena_filters(n, w1, b1, w2, b2, w3, b3, freq, w_out):
    t = jnp.linspace(0.0, 1.0, n, dtype=jnp.float32)[:, None]
    bands = (FILTER_EMB - 1) // 2
    w = 2.0 * math.pi * jnp.arange(n, dtype=jnp.float32)[:, None] / n
    f = jnp.linspace(1e-4, bands - 1, bands, dtype=jnp.float32)[None, :]
    z = jnp.concatenate([t, jnp.cos(f * w), -jnp.sin(f * w)], axis=-1)
    a = jnp.sin(freq * (z @ w1 + b1))
    a = jnp.sin(freq * (a @ w2 + b2))
    a = jnp.sin(freq * (a @ w3 + b3))
    h = (a @ w_out).astype(jnp.float32).reshape(n, HYENA_ORDER, FILTER_DIRS, HYENA_WIDTH)
    max_decay = math.log(DECAY_TARGET) / FAST_DECAY_PCT
    min_decay = math.log(DECAY_TARGET) / SLOW_DECAY_PCT
    deltas = jnp.linspace(min_decay, max_decay, HYENA_WIDTH, dtype=jnp.float32)
    decay = jnp.exp(-t * jnp.abs(deltas))
    return h * decay[:, None, None, :]


def long_conv(u, h_fwd, h_bwd, bias):
    n, ch = u.shape[1], u.shape[2]
    k = jnp.concatenate([h_fwd, jnp.zeros((1, ch), jnp.float32), h_bwd[:0:-1]], axis=0)
    u32 = u.astype(jnp.float32)
    y = jnp.fft.irfft(jnp.fft.rfft(u32, n=2 * n, axis=1) * jnp.fft.rfft(k, axis=0)[None],
                      n=2 * n, axis=1)[:, :n]
    return (y + u32 * bias.astype(jnp.float32)).astype(u.dtype)


def short_conv(u, w, b):
    y = lax.conv_general_dilated(u, w[:, None, :], window_strides=(1,),
                                 padding=[(SHORT_CONV // 2, SHORT_CONV // 2)],
                                 dimension_numbers=('NWC', 'WIO', 'NWC'),
                                 feature_group_count=u.shape[-1])
    return y + b


def hyena_mixer(p_hy, w_short, b_short, filters, hy_bias):
    u = short_conv(p_hy, w_short, b_short)
    v, x1, x2 = jnp.split(u, HYENA_ORDER + 1, axis=-1)
    z = x1 * long_conv(v, filters[:, 0, 0], filters[:, 0, 1], hy_bias[0])
    return x2 * long_conv(z, filters[:, 1, 0], filters[:, 1, 1], hy_bias[1])


def merge_branches(attn_o, hy_o, p_gates, w_br_attn, w_br_hyena, w_out):
    g_a, g_h = jnp.split(p_gates, 2, axis=-1)
    merged = jax.nn.sigmoid(g_a) * (attn_o @ w_br_attn) + jax.nn.sigmoid(g_h) * (hy_o @ w_br_hyena)
    return merged @ w_out


def swiglu_sublayer(x, shift, scale, gate, pre_g, post_g, w_gate, w_up, w_down):
    h = modulate(x, pre_g, shift, scale)
    f = (jax.nn.silu(h @ w_gate) * (h @ w_up)) @ w_down
    return x + MACARON_W * gate * rms_norm(f, post_g)


def setup_inputs(seed: int = 0) -> dict:
    key = jax.random.key(seed)
    ks = jax.random.split(key, 32)

    def nrm(k, shape, scale):
        return jax.random.normal(k, shape, jnp.float32) * scale

    D, L, HW = D_MODEL, DEPTH, HYENA_WIDTH
    return {
        "x": nrm(ks[0], (BATCH, SEQ, D), 1.0),
        "c": nrm(ks[1], (BATCH, D), 1.0),
        "ctx": nrm(ks[2], (BATCH, CTX_LEN, D), 1.0),
        "c_ctx": nrm(ks[3], (D,), 1.0),
        "w_ada": nrm(ks[4], (L, D, N_MOD * D), D ** -0.5),
        "b_ada": nrm(ks[5], (L, N_MOD * D), 0.01),
        "pre_g": 1.0 + nrm(ks[6], (L, 3, D), 0.02),
        "post_g": 1.0 + nrm(ks[7], (L, 3, D), 0.02),
        "ffn_w_gate": nrm(ks[8], (L, 2, D, D_FF), D ** -0.5),
        "ffn_w_up": nrm(ks[9], (L, 2, D, D_FF), D ** -0.5),
        "ffn_w_down": nrm(ks[10], (L, 2, D_FF, D), D_FF ** -0.5),
        "w_in": nrm(ks[11], (L, D, IN_COLS), D ** -0.5),
        "q_norm_g": 1.0 + nrm(ks[12], (L, HEAD_DIM), 0.02),
        "k_norm_g": 1.0 + nrm(ks[13], (L, HEAD_DIM), 0.02),
        "short_w": nrm(ks[14], (L, SHORT_CONV, (HYENA_ORDER + 1) * HW), SHORT_CONV ** -0.5),
        "short_b": nrm(ks[15], (L, (HYENA_ORDER + 1) * HW), 0.01),
        "filt_w1": nrm(ks[16], (L, FILTER_EMB, FILTER_HIDDEN), FILTER_EMB ** -0.5),
        "filt_b1": nrm(ks[17], (L, FILTER_HIDDEN), 0.01),
        "filt_w2": nrm(ks[18], (L, FILTER_HIDDEN, FILTER_HIDDEN), FILTER_HIDDEN ** -0.5),
        "filt_b2": nrm(ks[19], (L, FILTER_HIDDEN), 0.01),
        "filt_w3": nrm(ks[20], (L, FILTER_HIDDEN, FILTER_HIDDEN), FILTER_HIDDEN ** -0.5),
        "filt_b3": nrm(ks[21], (L, FILTER_HIDDEN), 0.01),
        "filt_freq": 1.0 + nrm(ks[22], (L, FILTER_HIDDEN), 0.02),
        "filt_w_out": nrm(ks[23], (L, FILTER_HIDDEN, HYENA_ORDER * FILTER_DIRS * HW),
                           FILTER_INIT_SCALE * FILTER_HIDDEN ** -0.5),
        "hyena_bias": nrm(ks[24], (L, HYENA_ORDER, HW), 0.1),
        "w_br_attn": nrm(ks[25], (L, ATTN_WIDTH, D), ATTN_WIDTH ** -0.5),
        "w_br_hyena": nrm(ks[26], (L, HW, D), HW ** -0.5),
        "w_out": nrm(ks[27], (L, D, D), D ** -0.5),
    }


def reference(x, c, ctx, c_ctx, w_ada, b_ada, pre_g, post_g, ffn_w_gate, ffn_w_up, ffn_w_down,
              w_in, q_norm_g, k_norm_g, short_w, short_b, filt_w1, filt_b1, filt_w2, filt_b2,
              filt_w3, filt_b3, filt_freq, filt_w_out, hyena_bias, w_br_attn, w_br_hyena, w_out):
    n_lat = x.shape[1]
    n_ctx = ctx.shape[1]
    row, col = grid_positions(n_lat)
    for l in range(DEPTH):
        last = l == DEPTH - 1
        m = (jax.nn.silu(c) @ w_ada[l] + b_ada[l]).reshape(c.shape[0], N_MOD, 1, D_MODEL)
        mc = (jax.nn.silu(c_ctx) @ w_ada[l] + b_ada[l]).reshape(N_MOD, D_MODEL)

        x = swiglu_sublayer(x, m[:, 0], m[:, 1], m[:, 2], pre_g[l, 0], post_g[l, 0],
                            ffn_w_gate[l, 0], ffn_w_up[l, 0], ffn_w_down[l, 0])
        ctx = swiglu_sublayer(ctx, mc[0], mc[1], mc[2], pre_g[l, 0], post_g[l, 0],
                              ffn_w_gate[l, 0], ffn_w_up[l, 0], ffn_w_down[l, 0])

        h = modulate(x, pre_g[l, 1], m[:, 3], m[:, 4])
        hc = modulate(ctx, pre_g[l, 1], mc[3], mc[4])
        p = h @ w_in[l]
        if last:
            pc_kv = hc @ w_in[l][:, Q_END:V_END]
        else:
            pc = hc @ w_in[l]
            pc_kv = pc[..., Q_END:V_END]
        kc, vc = kv_heads(pc_kv, k_norm_g[l])

        q = rope_2d(rms_norm(split_heads(p[..., :Q_END], N_Q_HEADS), q_norm_g[l]), row, col)
        k, v = kv_heads(p[..., Q_END:V_END], k_norm_g[l])
        k = rope_2d(k, row, col)
        k_all = jnp.concatenate([kc, k], axis=1)
        v_all = jnp.concatenate([vc, v], axis=1)
        attn_o = latent_attention(q, k_all, v_all)

        filt = (filt_w1[l], filt_b1[l], filt_w2[l], filt_b2[l], filt_w3[l], filt_b3[l],
                filt_freq[l], filt_w_out[l])
        hy_o = hyena_mixer(p[..., V_END:HY_END], short_w[l], short_b[l],
                           hyena_filters(n_lat, *filt), hyena_bias[l])

        out = merge_branches(attn_o, hy_o, p[..., HY_END:], w_br_attn[l], w_br_hyena[l], w_out[l])
        x = x + m[:, 5] * rms_norm(out, post_g[l, 1])

        if not last:
            qc = rms_norm(split_heads(pc[..., :Q_END], N_Q_HEADS), q_norm_g[l])
            attn_c = attend(qc.reshape(qc.shape[0], n_ctx, N_KV_HEADS, Q_PER_KV, HEAD_DIM), kc, vc)
            hy_c = hyena_mixer(pc[..., V_END:HY_END], short_w[l], short_b[l],
                               hyena_filters(n_ctx, *filt), hyena_bias[l])
            out_c = merge_branches(attn_c, hy_c, pc[..., HY_END:], w_br_attn[l], w_br_hyena[l], w_out[l])
            ctx = ctx + mc[5] * rms_norm(out_c, post_g[l, 1])
            ctx = swiglu_sublayer(ctx, mc[6], mc[7], mc[8], pre_g[l, 2], post_g[l, 2],
                                  ffn_w_gate[l, 1], ffn_w_up[l, 1], ffn_w_down[l, 1])

        x = swiglu_sublayer(x, m[:, 6], m[:, 7], m[:, 8], pre_g[l, 2], post_g[l, 2],
                            ffn_w_gate[l, 1], ffn_w_up[l, 1], ffn_w_down[l, 1])
    return x
```

```python
import functools
import math

import jax
import jax.numpy as jnp
from jax import lax
from jax.experimental import pallas as pl
from jax.experimental.pallas import tpu as pltpu

F32 = jnp.float32
BF16 = jnp.bfloat16

D_MODEL = 4096
GRID_W = 64
HEAD_DIM = 128
N_Q_HEADS = 16
N_KV_HEADS = 4
Q_PER_KV = N_Q_HEADS // N_KV_HEADS
ATTN_WIDTH = N_Q_HEADS * HEAD_DIM
KV_WIDTH = N_KV_HEADS * HEAD_DIM
ROPE_THETA = 10000.0
AXIS_DIM = HEAD_DIM // 2
HYENA_WIDTH = D_MODEL // 2
FILTER_EMB = 33
FILTER_HIDDEN = 64
DECAY_TARGET = 1e-2
FAST_DECAY_PCT = 0.3
SLOW_DECAY_PCT = 1.5
D_FF = 11008
MACARON_W = 0.5
N_MOD = 9
NORM_EPS = 1e-6
Q_END = ATTN_WIDTH
K_END = Q_END + KV_WIDTH
V_END = K_END + KV_WIDTH
HY_END = V_END + 3 * HYENA_WIDTH
IN_COLS = HY_END + 2 * D_MODEL

V7X_LANES = 128
V7X_VMEM_LIMIT_BYTES = 56 * 1024 * 1024

D_FF_PAD = 11264
ROW_TILE = 256
FILT_PAD = V7X_LANES


def _cparams(n_axes):
    return pltpu.CompilerParams(dimension_semantics=("arbitrary",) * n_axes,
                                vmem_limit_bytes=V7X_VMEM_LIMIT_BYTES)


def _rms(x):
    return x * lax.rsqrt(jnp.mean(x * x, axis=-1, keepdims=True) + NORM_EPS)


def _ada_kernel(c_ref, w_ref, b_ref, o_ref):
    c = c_ref[...]
    s = (c * jax.nn.sigmoid(c)).astype(BF16)
    o_ref[...] = jnp.dot(s, w_ref[...].astype(BF16), preferred_element_type=F32) + b_ref[...]


def _ada(cond, w, b):
    tn = 512
    n = w.shape[1]
    return pl.pallas_call(
        _ada_kernel,
        grid=(n // tn,),
        in_specs=[pl.BlockSpec((8, D_MODEL), lambda j: (0, 0)),
                  pl.BlockSpec((D_MODEL, tn), lambda j: (0, j)),
                  pl.BlockSpec((1, tn), lambda j: (0, j))],
        out_specs=pl.BlockSpec((8, tn), lambda j: (0, j)),
        out_shape=jax.ShapeDtypeStruct((8, n), F32),
        compiler_params=_cparams(1),
        name="ada",
    )(cond, w, b)


def _mod_index(i, rows_per_batch_tiles):
    return jnp.minimum(i // rows_per_batch_tiles, 4)


def _modulate_kernel(x_ref, c_ref, m_ref, g_ref, o_ref, *, nx, shift_i, scale_i):
    i = pl.program_id(0)

    def body(src):
        y = _rms(src[...]) * g_ref[...]
        o_ref[...] = (y * (1.0 + m_ref[0, scale_i:scale_i + 1, :])
                      + m_ref[0, shift_i:shift_i + 1, :]).astype(o_ref.dtype)

    pl.when(i < nx)(lambda: body(x_ref))
    pl.when(i >= nx)(lambda: body(c_ref))


def _modulate_first(x2d, c2d, mods, g, shift_i, scale_i):
    tr = ROW_TILE
    nx, nc = x2d.shape[0] // tr, c2d.shape[0] // tr
    per_batch = (x2d.shape[0] // 4) // tr
    return pl.pallas_call(
        functools.partial(_modulate_kernel, nx=nx, shift_i=shift_i, scale_i=scale_i),
        grid=(nx + nc,),
        in_specs=[pl.BlockSpec((tr, D_MODEL), lambda i: (jnp.minimum(i, nx - 1), 0)),
                  pl.BlockSpec((tr, D_MODEL), lambda i: (jnp.maximum(i - nx, 0), 0)),
                  pl.BlockSpec((1, N_MOD, D_MODEL), lambda i: (_mod_index(i, per_batch), 0, 0)),
                  pl.BlockSpec((1, D_MODEL), lambda i: (0, 0))],
        out_specs=pl.BlockSpec((tr, D_MODEL), lambda i: (i, 0)),
        out_shape=jax.ShapeDtypeStruct(((nx + nc) * tr, D_MODEL), BF16),
        compiler_params=_cparams(1),
        name="modulate_first",
    )(x2d, c2d, mods, g)


def _resid_kernel(*refs, nx, two_source, gate_i, weight, shift_i, scale_i, with_next):
    it = iter(refs)
    f_ref = next(it)
    x_ref = next(it)
    c_ref = next(it) if two_source else None
    m_ref = next(it)
    pg_ref = next(it)
    ng_ref = next(it) if with_next else None
    xo_ref = next(it)
    ho_ref = next(it) if with_next else None
    i = pl.program_id(0)

    def body(src):
        y = _rms(f_ref[...]) * pg_ref[...]
        xn = src[...] + (weight * m_ref[0, gate_i:gate_i + 1, :]) * y
        xo_ref[...] = xn
        if with_next:
            h = _rms(xn) * ng_ref[...]
            ho_ref[...] = (h * (1.0 + m_ref[0, scale_i:scale_i + 1, :])
                           + m_ref[0, shift_i:shift_i + 1, :]).astype(ho_ref.dtype)

    if two_source:
        pl.when(i < nx)(lambda: body(x_ref))
        pl.when(i >= nx)(lambda: body(c_ref))
    else:
        body(x_ref)


def _residual(f, x2d, c2d, mods, post_g, next_g, *, gate_i, weight, shift_i=0, scale_i=0):
    tr = ROW_TILE
    rows = f.shape[0]
    n = rows // tr
    two_source = c2d is not None
    with_next = next_g is not None
    nx = x2d.shape[0] // tr if two_source else n
    per_batch = (8192 // 4) // tr
    row = pl.BlockSpec((tr, D_MODEL), lambda i: (i, 0))
    vec = pl.BlockSpec((1, D_MODEL), lambda i: (0, 0))
    in_specs = [row]
    args = [f]
    if two_source:
        in_specs += [pl.BlockSpec((tr, D_MODEL), lambda i: (jnp.minimum(i, nx - 1), 0)),
                     pl.BlockSpec((tr, D_MODEL), lambda i: (jnp.maximum(i - nx, 0), 0))]
        args += [x2d, c2d]
    else:
        in_specs += [row]
        args += [x2d]
    in_specs += [pl.BlockSpec((1, N_MOD, D_MODEL), lambda i: (_mod_index(i, per_batch), 0, 0)), vec]
    args += [mods, post_g]
    out_specs = [row]
    out_shape = [jax.ShapeDtypeStruct((rows, D_MODEL), F32)]
    if with_next:
        in_specs += [vec]
        args += [next_g]
        out_specs += [row]
        out_shape += [jax.ShapeDtypeStruct((rows, D_MODEL), BF16)]
    return pl.pallas_call(
        functools.partial(_resid_kernel, nx=nx, two_source=two_source, gate_i=gate_i, weight=weight,
                          shift_i=shift_i, scale_i=scale_i, with_next=with_next),
        grid=(n,),
        in_specs=in_specs,
        out_specs=out_specs,
        out_shape=out_shape,
        compiler_params=_cparams(1),
        name="residual",
    )(*args)


def _mm_kernel(*refs, n_a, n_w, pairs, n_extra, n_out, nk, epilogue):
    a_refs = refs[:n_a]
    w_refs = refs[n_a:n_a + n_w]
    e_refs = refs[n_a + n_w:n_a + n_w + n_extra]
    o_refs = refs[n_a + n_w + n_extra:n_a + n_w + n_extra + n_out]
    acc_refs = refs[n_a + n_w + n_extra + n_out:]
    k = pl.program_id(2)
    prods = [jnp.dot(a_refs[ai][...], w_refs[wi][...], preferred_element_type=F32) for ai, wi in pairs]
    if nk == 1:
        epilogue(prods, e_refs, o_refs)
        return

    @pl.when(k == 0)
    def _():
        for acc, p in zip(acc_refs, prods):
            acc[...] = p

    @pl.when(k > 0)
    def _():
        for acc, p in zip(acc_refs, prods):
            acc[...] += p

    @pl.when(k == nk - 1)
    def _():
        epilogue([acc[...] for acc in acc_refs], e_refs, o_refs)


def _matmul(a_list, w_list, pairs, epilogue, out_cols, out_dtypes, *, tm, tn, tk, m_blocks, n_blocks,
            k_total, a_blk0=0, w_blk0=None, extras=(), extra_specs=(), name="matmul"):
    nk = k_total // tk
    w_blk0 = w_blk0 or [0] * len(w_list)
    in_specs = [pl.BlockSpec((tm, tk), lambda j, i, k: (i + a_blk0, k)) for _ in a_list]
    in_specs += [pl.BlockSpec((tk, tn), functools.partial(lambda j, i, k, o: (k, j + o), o=o)) for o in w_blk0]
    in_specs += list(extra_specs)
    out_specs = [pl.BlockSpec((tm, tn), lambda j, i, k: (i, j)) for _ in out_dtypes]
    out_shape = [jax.ShapeDtypeStruct((m_blocks * tm, out_cols), dt) for dt in out_dtypes]
    scratch = [pltpu.VMEM((tm, tn), F32) for _ in pairs] if nk > 1 else []
    return pl.pallas_call(
        functools.partial(_mm_kernel, n_a=len(a_list), n_w=len(w_list), pairs=tuple(pairs),
                          n_extra=len(extras), n_out=len(out_dtypes), nk=nk, epilogue=epilogue),
        grid=(n_blocks, m_blocks, nk),
        in_specs=in_specs,
        out_specs=out_specs,
        out_shape=out_shape,
        scratch_shapes=scratch,
        compiler_params=_cparams(3),
        name=name,
    )(*a_list, *w_list, *extras)


def _ep_store(accs, e_refs, o_refs):
    o_refs[0][...] = accs[0].astype(o_refs[0].dtype)


def _ep_swiglu(accs, e_refs, o_refs):
    g, u = accs
    o_refs[0][...] = (g * jax.nn.sigmoid(g) * u).astype(o_refs[0].dtype)


def _ep_merge(accs, e_refs, o_refs):
    a, h = accs
    ga, gh = e_refs
    o_refs[0][...] = (jax.nn.sigmoid(ga[...]) * a + jax.nn.sigmoid(gh[...]) * h).astype(o_refs[0].dtype)


def _rope_tables(n_lat):
    t = jnp.arange(n_lat, dtype=jnp.int32)
    row = (t // GRID_W).astype(F32)
    col = (t % GRID_W).astype(F32)
    inv_freq = ROPE_THETA ** (-jnp.arange(0, AXIS_DIM, 2, dtype=F32) / AXIS_DIM)
    ang = jnp.concatenate([jnp.tile(row[:, None] * inv_freq, (1, 2)),
                           jnp.tile(col[:, None] * inv_freq, (1, 2))], axis=-1)
    lane = jnp.arange(HEAD_DIM, dtype=jnp.int32)
    first = (lane % AXIS_DIM) < (AXIS_DIM // 2)
    cos = jnp.cos(ang)
    sin = jnp.sin(ang)
    sin_a = jnp.where(first, -sin, 0.0)
    sin_b = jnp.where(first, 0.0, sin)
    return cos, sin_a, sin_b


def _head_prep(x, g, cos, sin_a, sin_b, rope):
    y = _rms(x) * g
    if rope:
        up = pltpu.roll(y, HEAD_DIM - AXIS_DIM // 2, axis=1)
        dn = pltpu.roll(y, AXIS_DIM // 2, axis=1)
        y = y * cos + up * sin_a + dn * sin_b
    return y


def _qkv_prep_kernel(q_ref, kv_ref, qg_ref, kg_ref, cos_ref, sa_ref, sb_ref, qo_ref, ko_ref, vo_ref):
    cos, sa, sb = cos_ref[...], sa_ref[...], sb_ref[...]
    for h in range(N_Q_HEADS):
        sl = slice(h * HEAD_DIM, (h + 1) * HEAD_DIM)
        qo_ref[:, sl] = _head_prep(q_ref[:, sl], qg_ref[...], cos, sa, sb, True).astype(qo_ref.dtype)
    for h in range(N_KV_HEADS):
        sl = slice(h * HEAD_DIM, (h + 1) * HEAD_DIM)
        ko_ref[:, sl] = _head_prep(kv_ref[:, sl], kg_ref[...], cos, sa, sb, True).astype(ko_ref.dtype)
    vo_ref[...] = kv_ref[:, KV_WIDTH:].astype(vo_ref.dtype)


def _qkv_prep(p, q_g, k_g, tables):
    tr = ROW_TILE
    rows = p.shape[0]
    cos, sin_a, sin_b = tables
    nt = cos.shape[0] // tr
    tab = pl.BlockSpec((tr, HEAD_DIM), lambda i: (i % nt, 0))
    vec = pl.BlockSpec((1, HEAD_DIM), lambda i: (0, 0))
    kv_out = pl.BlockSpec((tr, KV_WIDTH), lambda i: (i, 0))
    return pl.pallas_call(
        _qkv_prep_kernel,
        grid=(rows // tr,),
        in_specs=[pl.BlockSpec((tr, ATTN_WIDTH), lambda i: (i, 0)),
                  pl.BlockSpec((tr, 2 * KV_WIDTH), lambda i: (i, Q_END // (2 * KV_WIDTH))),
                  vec, vec, tab, tab, tab],
        out_specs=[pl.BlockSpec((tr, ATTN_WIDTH), lambda i: (i, 0)), kv_out, kv_out],
        out_shape=[jax.ShapeDtypeStruct((rows, ATTN_WIDTH), BF16),
                   jax.ShapeDtypeStruct((rows, KV_WIDTH), BF16),
                   jax.ShapeDtypeStruct((rows, KV_WIDTH), BF16)],
        compiler_params=_cparams(1),
        name="qkv_prep",
    )(p, p, q_g, k_g, cos, sin_a, sin_b)


def _kv_prep_ctx_kernel(kv_ref, kg_ref, ko_ref, vo_ref):
    for h in range(N_KV_HEADS):
        sl = slice(h * HEAD_DIM, (h + 1) * HEAD_DIM)
        ko_ref[:, sl] = _head_prep(kv_ref[:, sl], kg_ref[...], None, None, None, False).astype(ko_ref.dtype)
    vo_ref[...] = kv_ref[:, KV_WIDTH:].astype(vo_ref.dtype)


def _kv_prep_ctx(pc_kv, k_g):
    tr = ROW_TILE
    rows = pc_kv.shape[0]
    kv_out = pl.BlockSpec((tr, KV_WIDTH), lambda i: (i, 0))
    return pl.pallas_call(
        _kv_prep_ctx_kernel,
        grid=(rows // tr,),
        in_specs=[pl.BlockSpec((tr, 2 * KV_WIDTH), lambda i: (i, 0)),
                  pl.BlockSpec((1, HEAD_DIM), lambda i: (0, 0))],
        out_specs=[kv_out, kv_out],
        out_shape=[jax.ShapeDtypeStruct((rows, KV_WIDTH), BF16)] * 2,
        compiler_params=_cparams(1),
        name="kv_prep_ctx",
    )(pc_kv, k_g)


def _attn_kernel(q_ref, k_ref, v_ref, o_ref):
    k = k_ref[...]
    v = v_ref[...]
    scale = 1.0 / math.sqrt(HEAD_DIM)
    for g in range(Q_PER_KV):
        sl = slice(g * HEAD_DIM, (g + 1) * HEAD_DIM)
        s = lax.dot_general(q_ref[:, sl], k, (((1,), (1,)), ((), ())), preferred_element_type=F32) * scale
        m = jnp.max(s, axis=-1, keepdims=True)
        p = jnp.exp(s - m)
        l = jnp.sum(p, axis=-1, keepdims=True)
        o = jnp.dot(p.astype(BF16), v, preferred_element_type=F32)
        o_ref[:, sl] = (o / l).astype(o_ref.dtype)


def _attention(q, k_all, v_all, n_lat):
    tq = 512
    b, t, _ = k_all.shape
    nq = n_lat // tq
    gw = Q_PER_KV * HEAD_DIM
    return pl.pallas_call(
        _attn_kernel,
        grid=(b, N_KV_HEADS, nq),
        in_specs=[pl.BlockSpec((tq, gw), lambda bi, h, i: (bi * nq + i, h)),
                  pl.BlockSpec((None, t, HEAD_DIM), lambda bi, h, i: (bi, 0, h)),
                  pl.BlockSpec((None, t, HEAD_DIM), lambda bi, h, i: (bi, 0, h))],
        out_specs=pl.BlockSpec((tq, gw), lambda bi, h, i: (bi * nq + i, h)),
        out_shape=jax.ShapeDtypeStruct(q.shape, BF16),
        compiler_params=_cparams(3),
        name="attention",
    )(q, k_all, v_all)


def _filter_kernel(z_ref, w1_ref, b1_ref, w2_ref, b2_ref, w3_ref, b3_ref, fr_ref, wo_ref, t_ref, ad_ref, o_ref,
                   a_ref):
    j = pl.program_id(0)
    hi = lax.Precision.HIGHEST

    @pl.when(j == 0)
    def _():
        fr = fr_ref[...]
        a = jnp.sin(fr * (jnp.dot(z_ref[...], w1_ref[...], precision=hi, preferred_element_type=F32)
                          + b1_ref[...]))
        a = jnp.sin(fr * (jnp.dot(a, w2_ref[...], precision=hi, preferred_element_type=F32) + b2_ref[...]))
        a_ref[...] = jnp.sin(fr * (jnp.dot(a, w3_ref[...], precision=hi, preferred_element_type=F32)
                                   + b3_ref[...]))

    h = jnp.dot(a_ref[...], wo_ref[...], precision=hi, preferred_element_type=F32)
    h = h * jnp.exp(-t_ref[...] * ad_ref[...])
    backward = (j // 2) % 2 == 1
    row = lax.broadcasted_iota(jnp.int32, h.shape, 0)
    o_ref[...] = jnp.where(jnp.logical_and(backward, row == 0), 0.0, h)


def _pad2(a, rows, cols):
    return jnp.pad(a, ((0, rows - a.shape[0]), (0, cols - a.shape[1])))


def _hyena_filters(n, w1, b1, w2, b2, w3, b3, freq, w_out):
    t = jnp.linspace(0.0, 1.0, n, dtype=F32)[:, None]
    bands = (FILTER_EMB - 1) // 2
    w = 2.0 * math.pi * jnp.arange(n, dtype=F32)[:, None] / n
    f = jnp.linspace(1e-4, bands - 1, bands, dtype=F32)[None, :]
    z = jnp.concatenate([t, jnp.cos(f * w), -jnp.sin(f * w)], axis=-1)
    max_decay = math.log(DECAY_TARGET) / FAST_DECAY_PCT
    min_decay = math.log(DECAY_TARGET) / SLOW_DECAY_PCT
    ad = jnp.abs(jnp.linspace(min_decay, max_decay, HYENA_WIDTH, dtype=F32))[None, :]
    p = FILT_PAD
    tn = 1024
    n_out = w_out.shape[1]
    full = lambda shape: pl.BlockSpec(shape, lambda j: (0, 0))
    return pl.pallas_call(
        _filter_kernel,
        grid=(n_out // tn,),
        in_specs=[full((n, p)), full((p, p)), full((1, p)), full((p, p)), full((1, p)), full((p, p)),
                  full((1, p)), full((1, p)),
                  pl.BlockSpec((p, tn), lambda j: (0, j)),
                  full((n, 1)),
                  pl.BlockSpec((1, tn), lambda j: (0, j % (HYENA_WIDTH // tn)))],
        out_specs=pl.BlockSpec((n, tn), lambda j: (0, j)),
        out_shape=jax.ShapeDtypeStruct((n, n_out), F32),
        scratch_shapes=[pltpu.VMEM((n, p), F32)],
        compiler_params=_cparams(1),
        name="hyena_filters",
    )(_pad2(z, n, p), _pad2(w1, p, p), _pad2(b1[None, :], 1, p), _pad2(w2, p, p), _pad2(b2[None, :], 1, p),
      _pad2(w3, p, p), _pad2(b3[None, :], 1, p), _pad2(freq[None, :], 1, p), _pad2(w_out, p, n_out), t, ad)


def _dft_matrices(n):
    big = 2 * n
    idx = jnp.arange(n, dtype=jnp.int32)
    ft = (idx[:, None] * idx[None, :]) % big
    ang = ft.astype(F32) * (2.0 * math.pi / big)
    cos = jnp.cos(ang)
    sin = jnp.sin(ang)
    alt = jnp.where(idx % 2 == 0, 1.0, -1.0).astype(F32)
    first = (idx == 0)
    f_re = cos
    f_im = jnp.where(first[:, None], alt[None, :], -sin)
    g_re = cos * jnp.where(first, 1.0 / big, 2.0 / big)[None, :]
    g_im = jnp.where(first[None, :], alt[:, None] / big, -sin * (2.0 / big))
    return f_re.astype(BF16), f_im.astype(BF16), g_re.astype(BF16), g_im.astype(BF16)


def _spectrum_kernel(fre_ref, fim_ref, hf_ref, hb_ref, kre_ref, kim_ref):
    i = pl.program_id(0)
    hf = hf_ref[...].astype(BF16)
    hb = hb_ref[...].astype(BF16)
    fre = fre_ref[...]
    fim = fim_ref[...]
    kre_ref[...] = (jnp.dot(fre, hf, preferred_element_type=F32) + jnp.dot(fre, hb, preferred_element_type=F32))
    d1 = jnp.dot(fim, hf, preferred_element_type=F32)
    d2 = jnp.dot(fim, hb, preferred_element_type=F32)
    row = lax.broadcasted_iota(jnp.int32, d1.shape, 0)
    nyq = jnp.logical_and(i == 0, row == 0)
    kim_ref[...] = jnp.where(nyq, d1 + d2, d1 - d2)


def _filter_spectra(h_time, f_re, f_im):
    n = h_time.shape[0]
    tf, tc = 1024, 512
    per = HYENA_WIDTH // tc
    return pl.pallas_call(
        _spectrum_kernel,
        grid=(n // tf, 2 * per),
        in_specs=[pl.BlockSpec((tf, n), lambda i, j: (i, 0)),
                  pl.BlockSpec((tf, n), lambda i, j: (i, 0)),
                  pl.BlockSpec((n, tc), lambda i, j: (0, (j // per) * 2 * per + j % per)),
                  pl.BlockSpec((n, tc), lambda i, j: (0, (j // per) * 2 * per + per + j % per))],
        out_specs=[pl.BlockSpec((tf, tc), lambda i, j: (i, j))] * 2,
        out_shape=[jax.ShapeDtypeStruct((n, 2 * HYENA_WIDTH), F32)] * 2,
        compiler_params=_cparams(2),
        name="filter_spectra",
    )(f_re, f_im, h_time, h_time)


def _short_conv_kernel(p_ref, w_ref, b_ref, u_ref, v16_ref):
    j = pl.program_id(1)
    x = p_ref[...]
    n = x.shape[0]
    row = lax.broadcasted_iota(jnp.int32, x.shape, 0)
    prev = jnp.where(row == 0, 0.0, pltpu.roll(x, 1, axis=0))
    nxt = jnp.where(row == n - 1, 0.0, pltpu.roll(x, n - 1, axis=0))
    u = prev * w_ref[0:1, :] + x * w_ref[1:2, :] + nxt * w_ref[2:3, :] + b_ref[...]
    u_ref[...] = u

    @pl.when(j < HYENA_WIDTH // u.shape[1])
    def _():
        v16_ref[...] = u.astype(v16_ref.dtype)


def _short_conv(p3, w, b):
    bsz, n, _ = p3.shape
    tc = 512
    nj = 3 * HYENA_WIDTH // tc
    nv = HYENA_WIDTH // tc
    off = V_END // tc
    return pl.pallas_call(
        _short_conv_kernel,
        grid=(bsz, nj),
        in_specs=[pl.BlockSpec((None, n, tc), lambda bi, j: (bi, 0, off + j)),
                  pl.BlockSpec((3, tc), lambda bi, j: (0, j)),
                  pl.BlockSpec((1, tc), lambda bi, j: (0, j))],
        out_specs=[pl.BlockSpec((None, n, tc), lambda bi, j: (bi, 0, j)),
                   pl.BlockSpec((None, n, tc), lambda bi, j: (bi, 0, jnp.minimum(j, nv - 1)))],
        out_shape=[jax.ShapeDtypeStruct((bsz, n, 3 * HYENA_WIDTH), F32),
                   jax.ShapeDtypeStruct((bsz, n, HYENA_WIDTH), BF16)],
        compiler_params=_cparams(2),
        name="short_conv",
    )(p3, w, b)


def _dft_fwd_kernel(fre_ref, fim_ref, u_ref, kre_ref, kim_ref, yre_ref, yim_ref):
    f = pl.program_id(2)
    u = u_ref[...]
    ure = jnp.dot(fre_ref[...], u, preferred_element_type=F32)
    uim = jnp.dot(fim_ref[...], u, preferred_element_type=F32)
    kre = kre_ref[...]
    kim = kim_ref[...]
    row = lax.broadcasted_iota(jnp.int32, ure.shape, 0)
    real_pair = jnp.logical_and(f == 0, row == 0)
    yre_ref[...] = jnp.where(real_pair, ure * kre, ure * kre - uim * kim).astype(yre_ref.dtype)
    yim_ref[...] = jnp.where(real_pair, uim * kim, ure * kim + uim * kre).astype(yim_ref.dtype)


def _dft_forward(u16, f_re, f_im, k_re, k_im, order):
    bsz, n, ch = u16.shape
    tf, tc = 1024, 512
    nc = ch // tc
    return pl.pallas_call(
        _dft_fwd_kernel,
        grid=(bsz, nc, n // tf),
        in_specs=[pl.BlockSpec((tf, n), lambda bi, c, f: (f, 0)),
                  pl.BlockSpec((tf, n), lambda bi, c, f: (f, 0)),
                  pl.BlockSpec((None, n, tc), lambda bi, c, f: (bi, 0, c)),
                  pl.BlockSpec((tf, tc), lambda bi, c, f: (f, order * nc + c)),
                  pl.BlockSpec((tf, tc), lambda bi, c, f: (f, order * nc + c))],
        out_specs=[pl.BlockSpec((None, tf, tc), lambda bi, c, f: (bi, f, c))] * 2,
        out_shape=[jax.ShapeDtypeStruct((bsz, n, ch), BF16)] * 2,
        compiler_params=_cparams(3),
        name="dft_forward",
    )(f_re, f_im, u16, k_re, k_im)


def _dft_inv_kernel(gre_ref, gim_ref, yre_ref, yim_ref, u_ref, gate_ref, bias_ref, *o_refs):
    y = (jnp.dot(gre_ref[...], yre_ref[...], preferred_element_type=F32)
         + jnp.dot(gim_ref[...], yim_ref[...], preferred_element_type=F32))
    z = gate_ref[...] * (y + u_ref[...] * bias_ref[...])
    for o in o_refs:
        o[...] = z.astype(o.dtype)


def _dft_inverse(y_re, y_im, g_re, g_im, u_arr, u_blk0, gate_arr, gate_blk0, bias, out_dtypes):
    bsz, n, ch = y_re.shape
    tt, tc = 1024, 512
    nc = ch // tc
    return pl.pallas_call(
        _dft_inv_kernel,
        grid=(bsz, nc, n // tt),
        in_specs=[pl.BlockSpec((tt, n), lambda bi, c, t: (t, 0)),
                  pl.BlockSpec((tt, n), lambda bi, c, t: (t, 0)),
                  pl.BlockSpec((None, n, tc), lambda bi, c, t: (bi, 0, c)),
                  pl.BlockSpec((None, n, tc), lambda bi, c, t: (bi, 0, c)),
                  pl.BlockSpec((None, tt, tc), lambda bi, c, t: (bi, t, u_blk0 * nc + c)),
                  pl.BlockSpec((None, tt, tc), lambda bi, c, t: (bi, t, gate_blk0 * nc + c)),
                  pl.BlockSpec((1, tc), lambda bi, c, t: (0, c))],
        out_specs=[pl.BlockSpec((None, tt, tc), lambda bi, c, t: (bi, t, c))] * len(out_dtypes),
        out_shape=[jax.ShapeDtypeStruct((bsz, n, ch), dt) for dt in out_dtypes],
        compiler_params=_cparams(3),
        name="dft_inverse",
    )(g_re, g_im, y_re, y_im, u_arr, gate_arr, bias)


def _ffn(h, w_gate, w_up, w_down, m_blocks):
    tm = 1024
    act = _matmul([h], [w_gate, w_up], [(0, 0), (0, 1)], _ep_swiglu, D_FF_PAD, [BF16],
                  tm=tm, tn=512, tk=D_MODEL, m_blocks=m_blocks, n_blocks=D_FF_PAD // 512, k_total=D_MODEL,
                  name="ffn_gate_up")[0]
    return _matmul([act], [w_down], [(0, 0)], _ep_store, D_MODEL, [F32],
                   tm=tm, tn=2048, tk=1024, m_blocks=m_blocks, n_blocks=D_MODEL // 2048, k_total=D_FF_PAD,
                   name="ffn_down")[0]


def kernel(x, c, ctx, c_ctx, w_ada, b_ada, pre_g, post_g, ffn_w_gate, ffn_w_up, ffn_w_down, w_in, q_norm_g,
           k_norm_g, short_w, short_b, filt_w1, filt_b1, filt_w2, filt_b2, filt_w3, filt_b3, filt_freq,
           filt_w_out, hyena_bias, w_br_attn, w_br_hyena, w_out):
    bsz, n_lat, d = x.shape
    n_ctx = ctx.shape[1]
    rows_x, rows_c = bsz * n_lat, bsz * n_ctx
    x2d = x.reshape(rows_x, d)
    c2d = ctx.reshape(rows_c, d)
    l = 0

    pad_ff = D_FF_PAD - D_FF
    wg = [jnp.pad(ffn_w_gate[l, s].astype(BF16), ((0, 0), (0, pad_ff))) for s in range(2)]
    wu = [jnp.pad(ffn_w_up[l, s].astype(BF16), ((0, 0), (0, pad_ff))) for s in range(2)]
    wd = [jnp.pad(ffn_w_down[l, s].astype(BF16), ((0, pad_ff), (0, 0))) for s in range(2)]
    w_in16 = w_in[l].astype(BF16)
    w_ba16 = w_br_attn[l].astype(BF16)
    w_bh16 = w_br_hyena[l].astype(BF16)
    w_out16 = w_out[l].astype(BF16)

    cond = jnp.concatenate([c, c_ctx[None, :], jnp.zeros((3, d), F32)], axis=0)
    mods = _ada(cond, w_ada[l], b_ada[l][None, :])[:5].reshape(5, N_MOD, d)
    pre = [pre_g[l, s][None, :] for s in range(3)]
    post = [post_g[l, s][None, :] for s in range(3)]

    h0 = _modulate_first(x2d, c2d, mods, pre[0], shift_i=0, scale_i=1)
    f1 = _ffn(h0, wg[0], wu[0], wd[0], (rows_x + rows_c) // 1024)
    x1, h1 = _residual(f1, x2d, c2d, mods, post[0], pre[1], gate_i=2, weight=MACARON_W, shift_i=3, scale_i=4)

    tm = 1024
    p = _matmul([h1], [w_in16], [(0, 0)], _ep_store, IN_COLS, [F32], tm=tm, tn=1024, tk=d,
                m_blocks=rows_x // tm, n_blocks=IN_COLS // 1024, k_total=d, name="in_proj")[0]
    pc_kv = _matmul([h1], [w_in16], [(0, 0)], _ep_store, 2 * KV_WIDTH, [F32], tm=tm, tn=1024, tk=d,
                    m_blocks=rows_c // tm, n_blocks=1, k_total=d, a_blk0=rows_x // tm,
                    w_blk0=[Q_END // 1024], name="in_proj_ctx")[0]

    tables = _rope_tables(n_lat)
    q16, k16, v16 = _qkv_prep(p, q_norm_g[l][None, :], k_norm_g[l][None, :], tables)
    kc16, vc16 = _kv_prep_ctx(pc_kv, k_norm_g[l][None, :])
    k_all = jnp.concatenate([kc16.reshape(bsz, n_ctx, KV_WIDTH), k16.reshape(bsz, n_lat, KV_WIDTH)], axis=1)
    v_all = jnp.concatenate([vc16.reshape(bsz, n_ctx, KV_WIDTH), v16.reshape(bsz, n_lat, KV_WIDTH)], axis=1)
    attn_o = _attention(q16, k_all, v_all, n_lat)

    f_re, f_im, g_re, g_im = _dft_matrices(n_lat)
    h_time = _hyena_filters(n_lat, filt_w1[l], filt_b1[l], filt_w2[l], filt_b2[l], filt_w3[l], filt_b3[l],
                            filt_freq[l], filt_w_out[l])
    k_re, k_im = _filter_spectra(h_time, f_re, f_im)
    p3 = p.reshape(bsz, n_lat, IN_COLS)
    u, v16h = _short_conv(p3, short_w[l], short_b[l][None, :])
    y_re, y_im = _dft_forward(v16h, f_re, f_im, k_re, k_im, order=0)
    z, z16 = _dft_inverse(y_re, y_im, g_re, g_im, u, 0, u, 1, hyena_bias[l, 0][None, :], [F32, BF16])
    y_re, y_im = _dft_forward(z16, f_re, f_im, k_re, k_im, order=1)
    hy_o = _dft_inverse(y_re, y_im, g_re, g_im, z, 0, u, 2, hyena_bias[l, 1][None, :], [BF16])[0]

    gate_spec = lambda off: pl.BlockSpec((tm, 512), lambda j, i, k: (i, off + j))
    merged = _matmul([attn_o, hy_o.reshape(rows_x, HYENA_WIDTH)], [w_ba16, w_bh16], [(0, 0), (1, 1)], _ep_merge,
                     d, [BF16], tm=tm, tn=512, tk=ATTN_WIDTH, m_blocks=rows_x // tm, n_blocks=d // 512,
                     k_total=ATTN_WIDTH, extras=(p, p),
                     extra_specs=(gate_spec(HY_END // 512), gate_spec((HY_END + d) // 512)), name="merge")[0]
    out = _matmul([merged], [w_out16], [(0, 0)], _ep_store, d, [F32], tm=tm, tn=1024, tk=d,
                  m_blocks=rows_x // tm, n_blocks=d // 1024, k_total=d, name="out_proj")[0]
    x2, h2 = _residual(out, x1, None, mods, post[1], pre[2], gate_i=5, weight=1.0, shift_i=6, scale_i=7)

    f2 = _ffn(h2, wg[1], wu[1], wd[1], rows_x // 1024)
    x3 = _residual(f2, x2, None, mods, post[2], None, gate_i=8, weight=MACARON_W)[0]
    return x3.reshape(bsz, n_lat, d)
```

```python
import functools
import math
from typing import NamedTuple

import jax
import jax.numpy as jnp
from jax import lax
from jax.experimental import pallas as pl
from jax.experimental.pallas import tpu as pltpu

F32 = jnp.float32
BF16 = jnp.bfloat16

D_MODEL = 4096
GRID_W = 64
HEAD_DIM = 128
N_Q_HEADS = 16
N_KV_HEADS = 4
Q_PER_KV = N_Q_HEADS // N_KV_HEADS
ATTN_WIDTH = N_Q_HEADS * HEAD_DIM
KV_WIDTH = N_KV_HEADS * HEAD_DIM
ROPE_THETA = 10000.0
AXIS_DIM = HEAD_DIM // 2
HYENA_WIDTH = D_MODEL // 2
FILTER_EMB = 33
FILTER_HIDDEN = 64
DECAY_TARGET = 1e-2
FAST_DECAY_PCT = 0.3
SLOW_DECAY_PCT = 1.5
D_FF = 11008
MACARON_W = 0.5
N_MOD = 9
NORM_EPS = 1e-6
Q_END = ATTN_WIDTH
K_END = Q_END + KV_WIDTH
V_END = K_END + KV_WIDTH
HY_END = V_END + 3 * HYENA_WIDTH
IN_COLS = HY_END + 2 * D_MODEL

V7X_LANES = 128
V7X_VMEM_LIMIT_BYTES = 56 * 1024 * 1024

MM_TM = 1024
MM_TN = 512
MM_CHUNK = 1024
DOWN_TM = 512
DOWN_CHUNK = D_FF // 16
ROW_TILE = 256
FILT_PAD = V7X_LANES


def _cparams(n_axes):
    return pltpu.CompilerParams(dimension_semantics=("arbitrary",) * n_axes,
                                vmem_limit_bytes=V7X_VMEM_LIMIT_BYTES)


def _rms(x):
    return x * lax.rsqrt(jnp.mean(x * x, axis=-1, keepdims=True) + NORM_EPS)


def _ada_kernel(c_ref, w_ref, b_ref, o_ref):
    c = c_ref[...]
    s = (c * jax.nn.sigmoid(c)).astype(BF16)
    o_ref[...] = jnp.dot(s, w_ref[...].astype(BF16), preferred_element_type=F32) + b_ref[...]


def _ada(cond, w, b, layer):
    tn = 512
    n = w.shape[2]
    return pl.pallas_call(
        _ada_kernel,
        grid=(n // tn,),
        in_specs=[pl.BlockSpec((8, D_MODEL), lambda j: (0, 0)),
                  pl.BlockSpec((None, D_MODEL, tn), lambda j: (layer, 0, j)),
                  pl.BlockSpec((1, tn), lambda j: (0, j))],
        out_specs=pl.BlockSpec((8, tn), lambda j: (0, j)),
        out_shape=jax.ShapeDtypeStruct((8, n), F32),
        compiler_params=_cparams(1),
        name="ada",
    )(cond, w, b)


def _mod_index(i, rows_per_batch_tiles):
    return jnp.minimum(i // rows_per_batch_tiles, 4)


def _modulate_kernel(x_ref, c_ref, m_ref, g_ref, o_ref, *, nx, shift_i, scale_i):
    i = pl.program_id(0)

    def body(src):
        y = _rms(src[...]) * g_ref[...]
        o_ref[...] = (y * (1.0 + m_ref[0, scale_i:scale_i + 1, :])
                      + m_ref[0, shift_i:shift_i + 1, :]).astype(o_ref.dtype)

    pl.when(i < nx)(lambda: body(x_ref))
    pl.when(i >= nx)(lambda: body(c_ref))


def _modulate_first(x2d, c2d, mods, g, shift_i, scale_i):
    tr = ROW_TILE
    nx, nc = x2d.shape[0] // tr, c2d.shape[0] // tr
    per_batch = (x2d.shape[0] // 4) // tr
    return pl.pallas_call(
        functools.partial(_modulate_kernel, nx=nx, shift_i=shift_i, scale_i=scale_i),
        grid=(nx + nc,),
        in_specs=[pl.BlockSpec((tr, D_MODEL), lambda i: (jnp.minimum(i, nx - 1), 0)),
                  pl.BlockSpec((tr, D_MODEL), lambda i: (jnp.maximum(i - nx, 0), 0)),
                  pl.BlockSpec((1, N_MOD, D_MODEL), lambda i: (_mod_index(i, per_batch), 0, 0)),
                  pl.BlockSpec((1, D_MODEL), lambda i: (0, 0))],
        out_specs=pl.BlockSpec((tr, D_MODEL), lambda i: (i, 0)),
        out_shape=jax.ShapeDtypeStruct(((nx + nc) * tr, D_MODEL), BF16),
        compiler_params=_cparams(1),
        name="modulate_first",
    )(x2d, c2d, mods, g)


def _resid_kernel(*refs, nx, two_source, gate_i, weight, shift_i, scale_i, with_next):
    it = iter(refs)
    f_ref = next(it)
    x_ref = next(it)
    c_ref = next(it) if two_source else None
    m_ref = next(it)
    pg_ref = next(it)
    ng_ref = next(it) if with_next else None
    xo_ref = next(it)
    ho_ref = next(it) if with_next else None
    i = pl.program_id(0)

    def body(src):
        y = _rms(f_ref[...]) * pg_ref[...]
        xn = src[...] + (weight * m_ref[0, gate_i:gate_i + 1, :]) * y
        xo_ref[...] = xn
        if with_next:
            h = _rms(xn) * ng_ref[...]
            ho_ref[...] = (h * (1.0 + m_ref[0, scale_i:scale_i + 1, :])
                           + m_ref[0, shift_i:shift_i + 1, :]).astype(ho_ref.dtype)

    if two_source:
        pl.when(i < nx)(lambda: body(x_ref))
        pl.when(i >= nx)(lambda: body(c_ref))
    else:
        body(x_ref)


def _residual(f, x2d, c2d, mods, post_g, next_g, *, gate_i, weight, shift_i=0, scale_i=0):
    tr = ROW_TILE
    rows = f.shape[0]
    n = rows // tr
    two_source = c2d is not None
    with_next = next_g is not None
    nx = x2d.shape[0] // tr if two_source else n
    per_batch = (8192 // 4) // tr
    row = pl.BlockSpec((tr, D_MODEL), lambda i: (i, 0))
    vec = pl.BlockSpec((1, D_MODEL), lambda i: (0, 0))
    in_specs = [row]
    args = [f]
    if two_source:
        in_specs += [pl.BlockSpec((tr, D_MODEL), lambda i: (jnp.minimum(i, nx - 1), 0)),
                     pl.BlockSpec((tr, D_MODEL), lambda i: (jnp.maximum(i - nx, 0), 0))]
        args += [x2d, c2d]
    else:
        in_specs += [row]
        args += [x2d]
    in_specs += [pl.BlockSpec((1, N_MOD, D_MODEL), lambda i: (_mod_index(i, per_batch), 0, 0)), vec]
    args += [mods, post_g]
    out_specs = [row]
    out_shape = [jax.ShapeDtypeStruct((rows, D_MODEL), F32)]
    if with_next:
        in_specs += [vec]
        args += [next_g]
        out_specs += [row]
        out_shape += [jax.ShapeDtypeStruct((rows, D_MODEL), BF16)]
    return pl.pallas_call(
        functools.partial(_resid_kernel, nx=nx, two_source=two_source, gate_i=gate_i, weight=weight,
                          shift_i=shift_i, scale_i=scale_i, with_next=with_next),
        grid=(n,),
        in_specs=in_specs,
        out_specs=out_specs,
        out_shape=out_shape,
        compiler_params=_cparams(1),
        name="residual",
    )(*args)


class _WeightPlan(NamedTuple):
    k: int
    ch: int
    tn: int
    n_tiles: int
    m_tiles: int
    col0: int
    last_width: int
    lead: tuple


def _wres_kernel(*refs, n_a, n_w, pairs, n_extra, n_out, epilogue, plan):
    a_refs = refs[:n_a]
    w_refs = refs[n_a:n_a + n_w]
    e_refs = refs[n_a + n_w:n_a + n_w + n_extra]
    o_refs = refs[n_a + n_w + n_extra:n_a + n_w + n_extra + n_out]
    wbf, stage, sem = refs[n_a + n_w + n_extra + n_out:]
    j = pl.program_id(0)
    i = pl.program_id(1)
    slot = j % 2
    per_w = plan.k // plan.ch
    n_chunks = n_w * per_w
    ragged = plan.last_width != plan.tn

    def chunk_copy(tile, c, width):
        w, r = divmod(c, per_w)
        col = plan.col0 + tile * plan.tn
        if not isinstance(col, int):
            col = pl.multiple_of(col, V7X_LANES)
        src = w_refs[w].at[(*plan.lead[w], pl.ds(r * plan.ch, plan.ch), pl.ds(col, width))]
        return pltpu.make_async_copy(src, stage.at[c % 2, :, pl.ds(0, width)], sem.at[c % 2])

    def chunk_unit(tile, dst_slot, c, width):
        w, r = divmod(c, per_w)
        chunk_copy(tile, c, width).wait()
        if c + 1 < n_chunks:
            chunk_copy(tile, c + 1, width).start()
        if c == 0 and width != plan.tn:
            for ww in range(n_w):
                wbf[dst_slot, ww, :, pl.ds(width, plan.tn - width)] = jnp.zeros((plan.k, plan.tn - width), BF16)
        wbf[dst_slot, w, pl.ds(r * plan.ch, plan.ch), pl.ds(0, width)] = (
            stage[c % 2, :, pl.ds(0, width)].astype(BF16))

    @pl.when(jnp.logical_and(j == 0, i == 0))
    def _():
        width = plan.last_width if plan.n_tiles == 1 else plan.tn
        chunk_copy(0, 0, width).start()
        for c in range(n_chunks):
            chunk_unit(0, 0, c, width)

    spread = max(plan.m_tiles - 1, 1)
    per_step = -(-n_chunks // spread)
    step_of = [min(plan.m_tiles - 1, 1 + c // per_step) for c in range(n_chunks)]
    variants = [(plan.tn, j + 1 < (plan.n_tiles - 1 if ragged else plan.n_tiles))]
    if ragged:
        variants.append((plan.last_width, j + 1 == plan.n_tiles - 1))
    for width, cond in variants:
        for s in sorted(set([0] + step_of)):
            @pl.when(jnp.logical_and(cond, i == s))
            def _(width=width, s=s):
                if s == 0:
                    chunk_copy(j + 1, 0, width).start()
                for c in range(n_chunks):
                    if step_of[c] == s:
                        chunk_unit(j + 1, 1 - slot, c, width)

    prods = [jnp.dot(a_refs[ai][...], wbf[slot, wi], preferred_element_type=F32) for ai, wi in pairs]
    epilogue(prods, e_refs, o_refs)


def _matmul_wres(a_list, w_list, pairs, epilogue, out_cols, out_dtypes, *, tm, tn, m_tiles, n_tiles, k, ch,
                 lead, col0=0, last_width=None, a_blk0=0, extras=(), extra_specs=(), name="matmul"):
    plan = _WeightPlan(k=k, ch=ch, tn=tn, n_tiles=n_tiles, m_tiles=m_tiles, col0=col0,
                       last_width=last_width or tn, lead=tuple(lead))
    in_specs = [pl.BlockSpec((tm, k), lambda j, i: (i + a_blk0, 0)) for _ in a_list]
    in_specs += [pl.BlockSpec(memory_space=pl.ANY) for _ in w_list]
    in_specs += list(extra_specs)
    return pl.pallas_call(
        functools.partial(_wres_kernel, n_a=len(a_list), n_w=len(w_list), pairs=tuple(pairs),
                          n_extra=len(extras), n_out=len(out_dtypes), epilogue=epilogue, plan=plan),
        grid=(n_tiles, m_tiles),
        in_specs=in_specs,
        out_specs=[pl.BlockSpec((tm, tn), lambda j, i: (i, j)) for _ in out_dtypes],
        out_shape=[jax.ShapeDtypeStruct((m_tiles * tm, out_cols), dt) for dt in out_dtypes],
        scratch_shapes=[pltpu.VMEM((2, len(w_list), k, tn), BF16),
                        pltpu.VMEM((2, ch, tn), F32),
                        pltpu.SemaphoreType.DMA((2,))],
        compiler_params=_cparams(2),
        name=name,
    )(*a_list, *w_list, *extras)


def _ep_store(accs, e_refs, o_refs):
    o_refs[0][...] = accs[0].astype(o_refs[0].dtype)


def _ep_swiglu(accs, e_refs, o_refs):
    g, u = accs
    o_refs[0][...] = (g * jax.nn.sigmoid(g) * u).astype(o_refs[0].dtype)


def _ep_merge(accs, e_refs, o_refs):
    a, h = accs
    ga, gh = e_refs
    o_refs[0][...] = (jax.nn.sigmoid(ga[...]) * a + jax.nn.sigmoid(gh[...]) * h).astype(o_refs[0].dtype)


def _rope_tables(n_lat):
    t = jnp.arange(n_lat, dtype=jnp.int32)
    row = (t // GRID_W).astype(F32)
    col = (t % GRID_W).astype(F32)
    inv_freq = ROPE_THETA ** (-jnp.arange(0, AXIS_DIM, 2, dtype=F32) / AXIS_DIM)
    ang = jnp.concatenate([jnp.tile(row[:, None] * inv_freq, (1, 2)),
                           jnp.tile(col[:, None] * inv_freq, (1, 2))], axis=-1)
    lane = jnp.arange(HEAD_DIM, dtype=jnp.int32)
    first = (lane % AXIS_DIM) < (AXIS_DIM // 2)
    cos = jnp.cos(ang)
    sin = jnp.sin(ang)
    sin_a = jnp.where(first, -sin, 0.0)
    sin_b = jnp.where(first, 0.0, sin)
    return cos, sin_a, sin_b


def _head_prep(x, g, cos, sin_a, sin_b, rope):
    y = _rms(x) * g
    if rope:
        up = pltpu.roll(y, HEAD_DIM - AXIS_DIM // 2, axis=1)
        dn = pltpu.roll(y, AXIS_DIM // 2, axis=1)
        y = y * cos + up * sin_a + dn * sin_b
    return y


def _qkv_prep_kernel(q_ref, kv_ref, qg_ref, kg_ref, cos_ref, sa_ref, sb_ref, qo_ref, ko_ref, vo_ref):
    cos, sa, sb = cos_ref[...], sa_ref[...], sb_ref[...]
    for h in range(N_Q_HEADS):
        sl = slice(h * HEAD_DIM, (h + 1) * HEAD_DIM)
        qo_ref[:, sl] = _head_prep(q_ref[:, sl], qg_ref[...], cos, sa, sb, True).astype(qo_ref.dtype)
    for h in range(N_KV_HEADS):
        sl = slice(h * HEAD_DIM, (h + 1) * HEAD_DIM)
        ko_ref[:, sl] = _head_prep(kv_ref[:, sl], kg_ref[...], cos, sa, sb, True).astype(ko_ref.dtype)
    vo_ref[...] = kv_ref[:, KV_WIDTH:].astype(vo_ref.dtype)


def _qkv_prep(p, q_g, k_g, tables):
    tr = ROW_TILE
    rows = p.shape[0]
    cos, sin_a, sin_b = tables
    nt = cos.shape[0] // tr
    tab = pl.BlockSpec((tr, HEAD_DIM), lambda i: (i % nt, 0))
    vec = pl.BlockSpec((1, HEAD_DIM), lambda i: (0, 0))
    kv_out = pl.BlockSpec((tr, KV_WIDTH), lambda i: (i, 0))
    return pl.pallas_call(
        _qkv_prep_kernel,
        grid=(rows // tr,),
        in_specs=[pl.BlockSpec((tr, ATTN_WIDTH), lambda i: (i, 0)),
                  pl.BlockSpec((tr, 2 * KV_WIDTH), lambda i: (i, Q_END // (2 * KV_WIDTH))),
                  vec, vec, tab, tab, tab],
        out_specs=[pl.BlockSpec((tr, ATTN_WIDTH), lambda i: (i, 0)), kv_out, kv_out],
        out_shape=[jax.ShapeDtypeStruct((rows, ATTN_WIDTH), BF16),
                   jax.ShapeDtypeStruct((rows, KV_WIDTH), BF16),
                   jax.ShapeDtypeStruct((rows, KV_WIDTH), BF16)],
        compiler_params=_cparams(1),
        name="qkv_prep",
    )(p, p, q_g, k_g, cos, sin_a, sin_b)


def _kv_prep_ctx_kernel(kv_ref, kg_ref, ko_ref, vo_ref):
    for h in range(N_KV_HEADS):
        sl = slice(h * HEAD_DIM, (h + 1) * HEAD_DIM)
        ko_ref[:, sl] = _head_prep(kv_ref[:, sl], kg_ref[...], None, None, None, False).astype(ko_ref.dtype)
    vo_ref[...] = kv_ref[:, KV_WIDTH:].astype(vo_ref.dtype)


def _kv_prep_ctx(pc_kv, k_g):
    tr = ROW_TILE
    rows = pc_kv.shape[0]
    kv_out = pl.BlockSpec((tr, KV_WIDTH), lambda i: (i, 0))
    return pl.pallas_call(
        _kv_prep_ctx_kernel,
        grid=(rows // tr,),
        in_specs=[pl.BlockSpec((tr, 2 * KV_WIDTH), lambda i: (i, 0)),
                  pl.BlockSpec((1, HEAD_DIM), lambda i: (0, 0))],
        out_specs=[kv_out, kv_out],
        out_shape=[jax.ShapeDtypeStruct((rows, KV_WIDTH), BF16)] * 2,
        compiler_params=_cparams(1),
        name="kv_prep_ctx",
    )(pc_kv, k_g)


def _attn_kernel(q_ref, k_ref, v_ref, o_ref):
    k = k_ref[...]
    v = v_ref[...]
    scale = 1.0 / math.sqrt(HEAD_DIM)
    for g in range(Q_PER_KV):
        sl = slice(g * HEAD_DIM, (g + 1) * HEAD_DIM)
        s = lax.dot_general(q_ref[:, sl], k, (((1,), (1,)), ((), ())), preferred_element_type=F32) * scale
        m = jnp.max(s, axis=-1, keepdims=True)
        p = jnp.exp(s - m)
        l = jnp.sum(p, axis=-1, keepdims=True)
        o = jnp.dot(p.astype(BF16), v, preferred_element_type=F32)
        o_ref[:, sl] = (o / l).astype(o_ref.dtype)


def _attention(q, k_all, v_all, n_lat):
    tq = 512
    b, t, _ = k_all.shape
    nq = n_lat // tq
    gw = Q_PER_KV * HEAD_DIM
    return pl.pallas_call(
        _attn_kernel,
        grid=(b, N_KV_HEADS, nq),
        in_specs=[pl.BlockSpec((tq, gw), lambda bi, h, i: (bi * nq + i, h)),
                  pl.BlockSpec((None, t, HEAD_DIM), lambda bi, h, i: (bi, 0, h)),
                  pl.BlockSpec((None, t, HEAD_DIM), lambda bi, h, i: (bi, 0, h))],
        out_specs=pl.BlockSpec((tq, gw), lambda bi, h, i: (bi * nq + i, h)),
        out_shape=jax.ShapeDtypeStruct(q.shape, BF16),
        compiler_params=_cparams(3),
        name="attention",
    )(q, k_all, v_all)


def _filter_kernel(z_ref, w1_ref, b1_ref, w2_ref, b2_ref, w3_ref, b3_ref, fr_ref, wo_ref, t_ref, ad_ref, o_ref,
                   a_ref):
    j = pl.program_id(0)
    hi = lax.Precision.HIGHEST

    @pl.when(j == 0)
    def _():
        fr = fr_ref[...]
        a = jnp.sin(fr * (jnp.dot(z_ref[...], w1_ref[...], precision=hi, preferred_element_type=F32)
                          + b1_ref[...]))
        a = jnp.sin(fr * (jnp.dot(a, w2_ref[...], precision=hi, preferred_element_type=F32) + b2_ref[...]))
        a_ref[...] = jnp.sin(fr * (jnp.dot(a, w3_ref[...], precision=hi, preferred_element_type=F32)
                                   + b3_ref[...]))

    h = jnp.dot(a_ref[...], wo_ref[...], precision=hi, preferred_element_type=F32)
    h = h * jnp.exp(-t_ref[...] * ad_ref[...])
    backward = (j // 2) % 2 == 1
    row = lax.broadcasted_iota(jnp.int32, h.shape, 0)
    o_ref[...] = jnp.where(jnp.logical_and(backward, row == 0), 0.0, h)


def _pad2(a, rows, cols):
    return jnp.pad(a, ((0, rows - a.shape[0]), (0, cols - a.shape[1])))


def _hyena_filters(n, w1, b1, w2, b2, w3, b3, freq, w_out):
    t = jnp.linspace(0.0, 1.0, n, dtype=F32)[:, None]
    bands = (FILTER_EMB - 1) // 2
    w = 2.0 * math.pi * jnp.arange(n, dtype=F32)[:, None] / n
    f = jnp.linspace(1e-4, bands - 1, bands, dtype=F32)[None, :]
    z = jnp.concatenate([t, jnp.cos(f * w), -jnp.sin(f * w)], axis=-1)
    max_decay = math.log(DECAY_TARGET) / FAST_DECAY_PCT
    min_decay = math.log(DECAY_TARGET) / SLOW_DECAY_PCT
    ad = jnp.abs(jnp.linspace(min_decay, max_decay, HYENA_WIDTH, dtype=F32))[None, :]
    p = FILT_PAD
    tn = 1024
    n_out = w_out.shape[1]
    full = lambda shape: pl.BlockSpec(shape, lambda j: (0, 0))
    return pl.pallas_call(
        _filter_kernel,
        grid=(n_out // tn,),
        in_specs=[full((n, p)), full((p, p)), full((1, p)), full((p, p)), full((1, p)), full((p, p)),
                  full((1, p)), full((1, p)),
                  pl.BlockSpec((p, tn), lambda j: (0, j)),
                  full((n, 1)),
                  pl.BlockSpec((1, tn), lambda j: (0, j % (HYENA_WIDTH // tn)))],
        out_specs=pl.BlockSpec((n, tn), lambda j: (0, j)),
        out_shape=jax.ShapeDtypeStruct((n, n_out), F32),
        scratch_shapes=[pltpu.VMEM((n, p), F32)],
        compiler_params=_cparams(1),
        name="hyena_filters",
    )(_pad2(z, n, p), _pad2(w1, p, p), _pad2(b1[None, :], 1, p), _pad2(w2, p, p), _pad2(b2[None, :], 1, p),
      _pad2(w3, p, p), _pad2(b3[None, :], 1, p), _pad2(freq[None, :], 1, p), _pad2(w_out, p, n_out), t, ad)


def _dft_matrices(n):
    big = 2 * n
    idx = jnp.arange(n, dtype=jnp.int32)
    ft = (idx[:, None] * idx[None, :]) % big
    ang = ft.astype(F32) * (2.0 * math.pi / big)
    cos = jnp.cos(ang)
    sin = jnp.sin(ang)
    alt = jnp.where(idx % 2 == 0, 1.0, -1.0).astype(F32)
    first = (idx == 0)
    f_re = cos
    f_im = jnp.where(first[:, None], alt[None, :], -sin)
    g_re = cos * jnp.where(first, 1.0 / big, 2.0 / big)[None, :]
    g_im = jnp.where(first[None, :], alt[:, None] / big, -sin * (2.0 / big))
    return f_re.astype(BF16), f_im.astype(BF16), g_re.astype(BF16), g_im.astype(BF16)


def _spectrum_kernel(fre_ref, fim_ref, hf_ref, hb_ref, kre_ref, kim_ref):
    i = pl.program_id(0)
    hf = hf_ref[...].astype(BF16)
    hb = hb_ref[...].astype(BF16)
    fre = fre_ref[...]
    fim = fim_ref[...]
    kre_ref[...] = (jnp.dot(fre, hf, preferred_element_type=F32) + jnp.dot(fre, hb, preferred_element_type=F32))
    d1 = jnp.dot(fim, hf, preferred_element_type=F32)
    d2 = jnp.dot(fim, hb, preferred_element_type=F32)
    row = lax.broadcasted_iota(jnp.int32, d1.shape, 0)
    nyq = jnp.logical_and(i == 0, row == 0)
    kim_ref[...] = jnp.where(nyq, d1 + d2, d1 - d2)


def _filter_spectra(h_time, f_re, f_im):
    n = h_time.shape[0]
    tf, tc = 1024, 512
    per = HYENA_WIDTH // tc
    return pl.pallas_call(
        _spectrum_kernel,
        grid=(n // tf, 2 * per),
        in_specs=[pl.BlockSpec((tf, n), lambda i, j: (i, 0)),
                  pl.BlockSpec((tf, n), lambda i, j: (i, 0)),
                  pl.BlockSpec((n, tc), lambda i, j: (0, (j // per) * 2 * per + j % per)),
                  pl.BlockSpec((n, tc), lambda i, j: (0, (j // per) * 2 * per + per + j % per))],
        out_specs=[pl.BlockSpec((tf, tc), lambda i, j: (i, j))] * 2,
        out_shape=[jax.ShapeDtypeStruct((n, 2 * HYENA_WIDTH), F32)] * 2,
        compiler_params=_cparams(2),
        name="filter_spectra",
    )(f_re, f_im, h_time, h_time)


def _short_conv_kernel(p_ref, w_ref, b_ref, u_ref, v16_ref):
    j = pl.program_id(1)
    x = p_ref[...]
    n = x.shape[0]
    row = lax.broadcasted_iota(jnp.int32, x.shape, 0)
    prev = jnp.where(row == 0, 0.0, pltpu.roll(x, 1, axis=0))
    nxt = jnp.where(row == n - 1, 0.0, pltpu.roll(x, n - 1, axis=0))
    u = prev * w_ref[0:1, :] + x * w_ref[1:2, :] + nxt * w_ref[2:3, :] + b_ref[...]
    u_ref[...] = u

    @pl.when(j < HYENA_WIDTH // u.shape[1])
    def _():
        v16_ref[...] = u.astype(v16_ref.dtype)


def _short_conv(p3, w, b):
    bsz, n, _ = p3.shape
    tc = 512
    nj = 3 * HYENA_WIDTH // tc
    nv = HYENA_WIDTH // tc
    off = V_END // tc
    return pl.pallas_call(
        _short_conv_kernel,
        grid=(bsz, nj),
        in_specs=[pl.BlockSpec((None, n, tc), lambda bi, j: (bi, 0, off + j)),
                  pl.BlockSpec((3, tc), lambda bi, j: (0, j)),
                  pl.BlockSpec((1, tc), lambda bi, j: (0, j))],
        out_specs=[pl.BlockSpec((None, n, tc), lambda bi, j: (bi, 0, j)),
                   pl.BlockSpec((None, n, tc), lambda bi, j: (bi, 0, jnp.minimum(j, nv - 1)))],
        out_shape=[jax.ShapeDtypeStruct((bsz, n, 3 * HYENA_WIDTH), F32),
                   jax.ShapeDtypeStruct((bsz, n, HYENA_WIDTH), BF16)],
        compiler_params=_cparams(2),
        name="short_conv",
    )(p3, w, b)


def _dft_fwd_kernel(fre_ref, fim_ref, u_ref, kre_ref, kim_ref, yre_ref, yim_ref):
    f = pl.program_id(2)
    u = u_ref[...]
    ure = jnp.dot(fre_ref[...], u, preferred_element_type=F32)
    uim = jnp.dot(fim_ref[...], u, preferred_element_type=F32)
    kre = kre_ref[...]
    kim = kim_ref[...]
    row = lax.broadcasted_iota(jnp.int32, ure.shape, 0)
    real_pair = jnp.logical_and(f == 0, row == 0)
    yre_ref[...] = jnp.where(real_pair, ure * kre, ure * kre - uim * kim).astype(yre_ref.dtype)
    yim_ref[...] = jnp.where(real_pair, uim * kim, ure * kim + uim * kre).astype(yim_ref.dtype)


def _dft_forward(u16, f_re, f_im, k_re, k_im, order):
    bsz, n, ch = u16.shape
    tf, tc = 1024, 512
    nc = ch // tc
    return pl.pallas_call(
        _dft_fwd_kernel,
        grid=(bsz, nc, n // tf),
        in_specs=[pl.BlockSpec((tf, n), lambda bi, c, f: (f, 0)),
                  pl.BlockSpec((tf, n), lambda bi, c, f: (f, 0)),
                  pl.BlockSpec((None, n, tc), lambda bi, c, f: (bi, 0, c)),
                  pl.BlockSpec((tf, tc), lambda bi, c, f: (f, order * nc + c)),
                  pl.BlockSpec((tf, tc), lambda bi, c, f: (f, order * nc + c))],
        out_specs=[pl.BlockSpec((None, tf, tc), lambda bi, c, f: (bi, f, c))] * 2,
        out_shape=[jax.ShapeDtypeStruct((bsz, n, ch), BF16)] * 2,
        compiler_params=_cparams(3),
        name="dft_forward",
    )(f_re, f_im, u16, k_re, k_im)


def _dft_inv_kernel(gre_ref, gim_ref, yre_ref, yim_ref, u_ref, gate_ref, bias_ref, *o_refs):
    y = (jnp.dot(gre_ref[...], yre_ref[...], preferred_element_type=F32)
         + jnp.dot(gim_ref[...], yim_ref[...], preferred_element_type=F32))
    z = gate_ref[...] * (y + u_ref[...] * bias_ref[...])
    for o in o_refs:
        o[...] = z.astype(o.dtype)


def _dft_inverse(y_re, y_im, g_re, g_im, u_arr, u_blk0, gate_arr, gate_blk0, bias, out_dtypes):
    bsz, n, ch = y_re.shape
    tt, tc = 1024, 512
    nc = ch // tc
    return pl.pallas_call(
        _dft_inv_kernel,
        grid=(bsz, nc, n // tt),
        in_specs=[pl.BlockSpec((tt, n), lambda bi, c, t: (t, 0)),
                  pl.BlockSpec((tt, n), lambda bi, c, t: (t, 0)),
                  pl.BlockSpec((None, n, tc), lambda bi, c, t: (bi, 0, c)),
                  pl.BlockSpec((None, n, tc), lambda bi, c, t: (bi, 0, c)),
                  pl.BlockSpec((None, tt, tc), lambda bi, c, t: (bi, t, u_blk0 * nc + c)),
                  pl.BlockSpec((None, tt, tc), lambda bi, c, t: (bi, t, gate_blk0 * nc + c)),
                  pl.BlockSpec((1, tc), lambda bi, c, t: (0, c))],
        out_specs=[pl.BlockSpec((None, tt, tc), lambda bi, c, t: (bi, t, c))] * len(out_dtypes),
        out_shape=[jax.ShapeDtypeStruct((bsz, n, ch), dt) for dt in out_dtypes],
        compiler_params=_cparams(3),
        name="dft_inverse",
    )(g_re, g_im, y_re, y_im, u_arr, gate_arr, bias)


def _ffn(h, w_gate, w_up, w_down, lead, rows):
    n_tiles = pl.cdiv(D_FF, MM_TN)
    act = _matmul_wres([h], [w_gate, w_up], [(0, 0), (0, 1)], _ep_swiglu, D_FF, [BF16], tm=MM_TM, tn=MM_TN,
                       m_tiles=rows // MM_TM, n_tiles=n_tiles, k=D_MODEL, ch=MM_CHUNK, lead=[lead, lead],
                       last_width=D_FF - (n_tiles - 1) * MM_TN, name="ffn_gate_up")[0]
    return _matmul_wres([act], [w_down], [(0, 0)], _ep_store, D_MODEL, [F32], tm=DOWN_TM, tn=MM_TN,
                        m_tiles=rows // DOWN_TM, n_tiles=D_MODEL // MM_TN, k=D_FF, ch=DOWN_CHUNK, lead=[lead],
                        name="ffn_down")[0]


def kernel(x, c, ctx, c_ctx, w_ada, b_ada, pre_g, post_g, ffn_w_gate, ffn_w_up, ffn_w_down, w_in, q_norm_g,
           k_norm_g, short_w, short_b, filt_w1, filt_b1, filt_w2, filt_b2, filt_w3, filt_b3, filt_freq,
           filt_w_out, hyena_bias, w_br_attn, w_br_hyena, w_out):
    bsz, n_lat, d = x.shape
    n_ctx = ctx.shape[1]
    rows_x, rows_c = bsz * n_lat, bsz * n_ctx
    x2d = x.reshape(rows_x, d)
    c2d = ctx.reshape(rows_c, d)
    l = 0

    cond = jnp.concatenate([c, c_ctx[None, :], jnp.zeros((3, d), F32)], axis=0)
    mods = _ada(cond, w_ada, b_ada[l][None, :], l)[:5].reshape(5, N_MOD, d)
    pre = [pre_g[l, s][None, :] for s in range(3)]
    post = [post_g[l, s][None, :] for s in range(3)]

    h0 = _modulate_first(x2d, c2d, mods, pre[0], shift_i=0, scale_i=1)
    f1 = _ffn(h0, ffn_w_gate, ffn_w_up, ffn_w_down, (l, 0), rows_x + rows_c)
    x1, h1 = _residual(f1, x2d, c2d, mods, post[0], pre[1], gate_i=2, weight=MACARON_W, shift_i=3, scale_i=4)

    tm, tn = MM_TM, MM_TN
    p = _matmul_wres([h1], [w_in], [(0, 0)], _ep_store, IN_COLS, [F32], tm=tm, tn=tn, m_tiles=rows_x // tm,
                     n_tiles=IN_COLS // tn, k=d, ch=MM_CHUNK, lead=[(l,)], name="in_proj")[0]
    pc_kv = _matmul_wres([h1], [w_in], [(0, 0)], _ep_store, 2 * KV_WIDTH, [F32], tm=tm, tn=tn,
                         m_tiles=rows_c // tm, n_tiles=2 * KV_WIDTH // tn, k=d, ch=MM_CHUNK, lead=[(l,)],
                         col0=Q_END, a_blk0=rows_x // tm, name="in_proj_ctx")[0]

    tables = _rope_tables(n_lat)
    q16, k16, v16 = _qkv_prep(p, q_norm_g[l][None, :], k_norm_g[l][None, :], tables)
    kc16, vc16 = _kv_prep_ctx(pc_kv, k_norm_g[l][None, :])
    k_all = jnp.concatenate([kc16.reshape(bsz, n_ctx, KV_WIDTH), k16.reshape(bsz, n_lat, KV_WIDTH)], axis=1)
    v_all = jnp.concatenate([vc16.reshape(bsz, n_ctx, KV_WIDTH), v16.reshape(bsz, n_lat, KV_WIDTH)], axis=1)
    attn_o = _attention(q16, k_all, v_all, n_lat)

    f_re, f_im, g_re, g_im = _dft_matrices(n_lat)
    h_time = _hyena_filters(n_lat, filt_w1[l], filt_b1[l], filt_w2[l], filt_b2[l], filt_w3[l], filt_b3[l],
                            filt_freq[l], filt_w_out[l])
    k_re, k_im = _filter_spectra(h_time, f_re, f_im)
    p3 = p.reshape(bsz, n_lat, IN_COLS)
    u, v16h = _short_conv(p3, short_w[l], short_b[l][None, :])
    y_re, y_im = _dft_forward(v16h, f_re, f_im, k_re, k_im, order=0)
    z, z16 = _dft_inverse(y_re, y_im, g_re, g_im, u, 0, u, 1, hyena_bias[l, 0][None, :], [F32, BF16])
    y_re, y_im = _dft_forward(z16, f_re, f_im, k_re, k_im, order=1)
    hy_o = _dft_inverse(y_re, y_im, g_re, g_im, z, 0, u, 2, hyena_bias[l, 1][None, :], [BF16])[0]

    gate_spec = lambda off: pl.BlockSpec((tm, tn), lambda j, i: (i, off + j))
    merged = _matmul_wres([attn_o, hy_o.reshape(rows_x, HYENA_WIDTH)], [w_br_attn, w_br_hyena], [(0, 0), (1, 1)],
                          _ep_merge, d, [BF16], tm=tm, tn=tn, m_tiles=rows_x // tm, n_tiles=d // tn,
                          k=ATTN_WIDTH, ch=MM_CHUNK, lead=[(l,), (l,)], extras=(p, p),
                          extra_specs=(gate_spec(HY_END // tn), gate_spec((HY_END + d) // tn)), name="merge")[0]
    out = _matmul_wres([merged], [w_out], [(0, 0)], _ep_store, d, [F32], tm=tm, tn=tn, m_tiles=rows_x // tm,
                       n_tiles=d // tn, k=d, ch=MM_CHUNK, lead=[(l,)], name="out_proj")[0]
    x2, h2 = _residual(out, x1, None, mods, post[1], pre[2], gate_i=5, weight=1.0, shift_i=6, scale_i=7)

    f2 = _ffn(h2, ffn_w_gate, ffn_w_up, ffn_w_down, (l, 1), rows_x)
    x3 = _residual(f2, x2, None, mods, post[2], None, gate_i=8, weight=MACARON_W)[0]
    return x3.reshape(bsz, n_lat, d)
```

```python
import functools
import math
from typing import NamedTuple

import jax
import jax.numpy as jnp
from jax import lax
from jax.experimental import pallas as pl
from jax.experimental.pallas import tpu as pltpu

F32 = jnp.float32
BF16 = jnp.bfloat16

D_MODEL = 4096
GRID_W = 64
HEAD_DIM = 128
N_Q_HEADS = 16
N_KV_HEADS = 4
Q_PER_KV = N_Q_HEADS // N_KV_HEADS
ATTN_WIDTH = N_Q_HEADS * HEAD_DIM
KV_WIDTH = N_KV_HEADS * HEAD_DIM
ROPE_THETA = 10000.0
AXIS_DIM = HEAD_DIM // 2
HYENA_WIDTH = D_MODEL // 2
FILTER_EMB = 33
FILTER_HIDDEN = 64
DECAY_TARGET = 1e-2
FAST_DECAY_PCT = 0.3
SLOW_DECAY_PCT = 1.5
D_FF = 11008
MACARON_W = 0.5
N_MOD = 9
NORM_EPS = 1e-6
Q_END = ATTN_WIDTH
K_END = Q_END + KV_WIDTH
V_END = K_END + KV_WIDTH
HY_END = V_END + 3 * HYENA_WIDTH
IN_COLS = HY_END + 2 * D_MODEL
SCORE_SCALE_LOG2 = math.log2(math.e) / math.sqrt(HEAD_DIM)

V7X_LANES = 128
BF16_SUBLANES = 16
V7X_VMEM_LIMIT_BYTES = 56 * 1024 * 1024

MM_TM = 1024
MM_TN = 512
MM_CHUNK = 1024
CTX_TM = 256
DOWN_TM = 512
DOWN_CHUNK = D_FF // 16
ROW_TILE = 256
DFT_ROWS = 64
FILT_PAD = V7X_LANES


def _cparams(n_axes):
    return pltpu.CompilerParams(dimension_semantics=("arbitrary",) * n_axes,
                                vmem_limit_bytes=V7X_VMEM_LIMIT_BYTES)


def _rms(x):
    return x * lax.rsqrt(jnp.mean(x * x, axis=-1, keepdims=True) + NORM_EPS)


def _ada_kernel(c_ref, w_ref, b_ref, o_ref):
    c = c_ref[...]
    s = (c * jax.nn.sigmoid(c)).astype(BF16)
    o_ref[...] = jnp.dot(s, w_ref[...].astype(BF16), preferred_element_type=F32) + b_ref[...]


def _ada(cond, w, b, layer):
    tn = 512
    n = w.shape[2]
    return pl.pallas_call(
        _ada_kernel,
        grid=(n // tn,),
        in_specs=[pl.BlockSpec((8, D_MODEL), lambda j: (0, 0)),
                  pl.BlockSpec((None, D_MODEL, tn), lambda j: (layer, 0, j)),
                  pl.BlockSpec((1, tn), lambda j: (0, j))],
        out_specs=pl.BlockSpec((8, tn), lambda j: (0, j)),
        out_shape=jax.ShapeDtypeStruct((8, n), F32),
        compiler_params=_cparams(1),
        name="ada",
    )(cond, w, b)


def _mod_index(i, rows_per_batch_tiles):
    return jnp.minimum(i // rows_per_batch_tiles, 4)


def _modulate_kernel(x_ref, c_ref, m_ref, g_ref, o_ref, *, nx, shift_i, scale_i):
    i = pl.program_id(0)

    def body(src):
        y = _rms(src[...]) * g_ref[...]
        o_ref[...] = (y * (1.0 + m_ref[0, scale_i:scale_i + 1, :])
                      + m_ref[0, shift_i:shift_i + 1, :]).astype(o_ref.dtype)

    pl.when(i < nx)(lambda: body(x_ref))
    pl.when(i >= nx)(lambda: body(c_ref))


def _modulate_first(x2d, c2d, mods, g, shift_i, scale_i):
    tr = ROW_TILE
    nx, nc = x2d.shape[0] // tr, c2d.shape[0] // tr
    per_batch = (x2d.shape[0] // 4) // tr
    return pl.pallas_call(
        functools.partial(_modulate_kernel, nx=nx, shift_i=shift_i, scale_i=scale_i),
        grid=(nx + nc,),
        in_specs=[pl.BlockSpec((tr, D_MODEL), lambda i: (jnp.minimum(i, nx - 1), 0)),
                  pl.BlockSpec((tr, D_MODEL), lambda i: (jnp.maximum(i - nx, 0), 0)),
                  pl.BlockSpec((1, N_MOD, D_MODEL), lambda i: (_mod_index(i, per_batch), 0, 0)),
                  pl.BlockSpec((1, D_MODEL), lambda i: (0, 0))],
        out_specs=pl.BlockSpec((tr, D_MODEL), lambda i: (i, 0)),
        out_shape=jax.ShapeDtypeStruct(((nx + nc) * tr, D_MODEL), BF16),
        compiler_params=_cparams(1),
        name="modulate_first",
    )(x2d, c2d, mods, g)


def _resid_kernel(*refs, nx, two_source, gate_i, weight, shift_i, scale_i, with_next):
    it = iter(refs)
    f_ref = next(it)
    x_ref = next(it)
    c_ref = next(it) if two_source else None
    m_ref = next(it)
    pg_ref = next(it)
    ng_ref = next(it) if with_next else None
    xo_ref = next(it)
    ho_ref = next(it) if with_next else None
    i = pl.program_id(0)

    def body(src):
        y = _rms(f_ref[...]) * pg_ref[...]
        xn = src[...] + (weight * m_ref[0, gate_i:gate_i + 1, :]) * y
        xo_ref[...] = xn
        if with_next:
            h = _rms(xn) * ng_ref[...]
            ho_ref[...] = (h * (1.0 + m_ref[0, scale_i:scale_i + 1, :])
                           + m_ref[0, shift_i:shift_i + 1, :]).astype(ho_ref.dtype)

    if two_source:
        pl.when(i < nx)(lambda: body(x_ref))
        pl.when(i >= nx)(lambda: body(c_ref))
    else:
        body(x_ref)


def _residual(f, x2d, c2d, mods, post_g, next_g, *, gate_i, weight, shift_i=0, scale_i=0):
    tr = ROW_TILE
    rows = f.shape[0]
    n = rows // tr
    two_source = c2d is not None
    with_next = next_g is not None
    nx = x2d.shape[0] // tr if two_source else n
    per_batch = (8192 // 4) // tr
    row = pl.BlockSpec((tr, D_MODEL), lambda i: (i, 0))
    vec = pl.BlockSpec((1, D_MODEL), lambda i: (0, 0))
    in_specs = [row]
    args = [f]
    if two_source:
        in_specs += [pl.BlockSpec((tr, D_MODEL), lambda i: (jnp.minimum(i, nx - 1), 0)),
                     pl.BlockSpec((tr, D_MODEL), lambda i: (jnp.maximum(i - nx, 0), 0))]
        args += [x2d, c2d]
    else:
        in_specs += [row]
        args += [x2d]
    in_specs += [pl.BlockSpec((1, N_MOD, D_MODEL), lambda i: (_mod_index(i, per_batch), 0, 0)), vec]
    args += [mods, post_g]
    out_specs = [row]
    out_shape = [jax.ShapeDtypeStruct((rows, D_MODEL), F32)]
    if with_next:
        in_specs += [vec]
        args += [next_g]
        out_specs += [row]
        out_shape += [jax.ShapeDtypeStruct((rows, D_MODEL), BF16)]
    return pl.pallas_call(
        functools.partial(_resid_kernel, nx=nx, two_source=two_source, gate_i=gate_i, weight=weight,
                          shift_i=shift_i, scale_i=scale_i, with_next=with_next),
        grid=(n,),
        in_specs=in_specs,
        out_specs=out_specs,
        out_shape=out_shape,
        compiler_params=_cparams(1),
        name="residual",
    )(*args)


class _WeightPlan(NamedTuple):
    k: int
    ch: int
    tn: int
    n_tiles: int
    m_tiles: int
    col0: int
    last_width: int
    lead: tuple


def _wres_kernel(*refs, n_a, n_w, pairs, n_extra, n_out, epilogue, plan):
    a_refs = refs[:n_a]
    w_refs = refs[n_a:n_a + n_w]
    e_refs = refs[n_a + n_w:n_a + n_w + n_extra]
    o_refs = refs[n_a + n_w + n_extra:n_a + n_w + n_extra + n_out]
    wbf, stage, sem = refs[n_a + n_w + n_extra + n_out:]
    j = pl.program_id(0)
    i = pl.program_id(1)
    slot = j % 2
    per_w = plan.k // plan.ch
    n_chunks = n_w * per_w
    ragged = plan.last_width != plan.tn

    def chunk_copy(tile, c, width):
        w, r = divmod(c, per_w)
        col = plan.col0 + tile * plan.tn
        if not isinstance(col, int):
            col = pl.multiple_of(col, V7X_LANES)
        src = w_refs[w].at[(*plan.lead[w], pl.ds(r * plan.ch, plan.ch), pl.ds(col, width))]
        return pltpu.make_async_copy(src, stage.at[c % 2, :, pl.ds(0, width)], sem.at[c % 2])

    def width_variants(tile):
        if not ragged:
            return [(plan.tn, tile < plan.n_tiles)]
        return [(plan.tn, tile < plan.n_tiles - 1), (plan.last_width, tile == plan.n_tiles - 1)]

    @pl.when(jnp.logical_and(j == 0, i == 0))
    def _():
        chunk_copy(0, 0, plan.tn).start()
        for c in range(n_chunks):
            w, r = divmod(c, per_w)
            chunk_copy(0, c, plan.tn).wait()
            if c + 1 < n_chunks:
                chunk_copy(0, c + 1, plan.tn).start()
            wbf[0, w, pl.ds(r * plan.ch, plan.ch), :] = stage[c % 2].astype(BF16)
        second = plan.last_width if plan.n_tiles == 2 else plan.tn
        chunk_copy(1, 0, second).start()

    for width, cond in width_variants(j + 1):
        for c in range(n_chunks):
            @pl.when(jnp.logical_and(cond, i == c))
            def _(width=width, c=c):
                chunk_copy(j + 1, c, width).wait()
                if c + 1 < n_chunks:
                    chunk_copy(j + 1, c + 1, width).start()

    for width, cond in width_variants(j + 2):
        @pl.when(jnp.logical_and(cond, i == plan.m_tiles - 1))
        def _(width=width):
            chunk_copy(j + 2, 0, width).start()

    c_now = jnp.minimum(i, n_chunks - 1)
    w_now = c_now // per_w
    row_now = pl.multiple_of((c_now % per_w) * plan.ch, BF16_SUBLANES)
    wbf[1 - slot, w_now, pl.ds(row_now, plan.ch), :] = stage[c_now % 2].astype(BF16)

    prods = [jnp.dot(a_refs[ai][...], wbf[slot, wi], preferred_element_type=F32) for ai, wi in pairs]
    epilogue(prods, e_refs, o_refs)


def _matmul_wres(a_list, w_list, pairs, epilogue, out_cols, out_dtypes, *, tm, tn, m_tiles, n_tiles, k, ch,
                 lead, col0=0, last_width=None, a_blk0=0, extras=(), extra_specs=(), name="matmul"):
    plan = _WeightPlan(k=k, ch=ch, tn=tn, n_tiles=n_tiles, m_tiles=m_tiles, col0=col0,
                       last_width=last_width or tn, lead=tuple(lead))
    n_chunks = len(w_list) * (k // ch)
    assert n_chunks <= m_tiles and n_chunks % 2 == 0 and n_tiles >= 2 and k % ch == 0
    in_specs = [pl.BlockSpec((tm, k), lambda j, i: (i + a_blk0, 0)) for _ in a_list]
    in_specs += [pl.BlockSpec(memory_space=pl.ANY) for _ in w_list]
    in_specs += list(extra_specs)
    return pl.pallas_call(
        functools.partial(_wres_kernel, n_a=len(a_list), n_w=len(w_list), pairs=tuple(pairs),
                          n_extra=len(extras), n_out=len(out_dtypes), epilogue=epilogue, plan=plan),
        grid=(n_tiles, m_tiles),
        in_specs=in_specs,
        out_specs=[pl.BlockSpec((tm, tn), lambda j, i: (i, j)) for _ in out_dtypes],
        out_shape=[jax.ShapeDtypeStruct((m_tiles * tm, out_cols), dt) for dt in out_dtypes],
        scratch_shapes=[pltpu.VMEM((2, len(w_list), k, tn), BF16),
                        pltpu.VMEM((2, ch, tn), F32),
                        pltpu.SemaphoreType.DMA((2,))],
        compiler_params=_cparams(2),
        name=name,
    )(*a_list, *w_list, *extras)


def _ep_store(accs, e_refs, o_refs):
    o_refs[0][...] = accs[0].astype(o_refs[0].dtype)


def _ep_swiglu(accs, e_refs, o_refs):
    g, u = accs
    o_refs[0][...] = (g * jax.nn.sigmoid(g) * u).astype(o_refs[0].dtype)


def _ep_merge(accs, e_refs, o_refs):
    a, h = accs
    ga, gh = e_refs
    o_refs[0][...] = (jax.nn.sigmoid(ga[...]) * a + jax.nn.sigmoid(gh[...]) * h).astype(o_refs[0].dtype)


def _rope_tables(n_lat):
    t = jnp.arange(n_lat, dtype=jnp.int32)
    row = (t // GRID_W).astype(F32)
    col = (t % GRID_W).astype(F32)
    inv_freq = ROPE_THETA ** (-jnp.arange(0, AXIS_DIM, 2, dtype=F32) / AXIS_DIM)
    ang = jnp.concatenate([jnp.tile(row[:, None] * inv_freq, (1, 2)),
                           jnp.tile(col[:, None] * inv_freq, (1, 2))], axis=-1)
    lane = jnp.arange(HEAD_DIM, dtype=jnp.int32)
    first = (lane % AXIS_DIM) < (AXIS_DIM // 2)
    cos = jnp.cos(ang)
    sin = jnp.sin(ang)
    sin_a = jnp.where(first, -sin, 0.0)
    sin_b = jnp.where(first, 0.0, sin)
    return cos, sin_a, sin_b


def _head_prep(x, g, cos, sin_a, sin_b, rope):
    y = _rms(x) * g
    if rope:
        up = pltpu.roll(y, HEAD_DIM - AXIS_DIM // 2, axis=1)
        dn = pltpu.roll(y, AXIS_DIM // 2, axis=1)
        y = y * cos + up * sin_a + dn * sin_b
    return y


def _qkv_prep_kernel(q_ref, kv_ref, qg_ref, kg_ref, cos_ref, sa_ref, sb_ref, qo_ref, ko_ref, vo_ref):
    cos, sa, sb = cos_ref[...], sa_ref[...], sb_ref[...]
    for h in range(N_Q_HEADS):
        sl = slice(h * HEAD_DIM, (h + 1) * HEAD_DIM)
        q = _head_prep(q_ref[:, sl], qg_ref[...], cos, sa, sb, True)
        qo_ref[:, sl] = (q * SCORE_SCALE_LOG2).astype(qo_ref.dtype)
    for h in range(N_KV_HEADS):
        sl = slice(h * HEAD_DIM, (h + 1) * HEAD_DIM)
        ko_ref[:, sl] = _head_prep(kv_ref[:, sl], kg_ref[...], cos, sa, sb, True).astype(ko_ref.dtype)
    vo_ref[...] = kv_ref[:, KV_WIDTH:].astype(vo_ref.dtype)


def _qkv_prep(p, q_g, k_g, tables):
    tr = ROW_TILE
    rows = p.shape[0]
    cos, sin_a, sin_b = tables
    nt = cos.shape[0] // tr
    tab = pl.BlockSpec((tr, HEAD_DIM), lambda i: (i % nt, 0))
    vec = pl.BlockSpec((1, HEAD_DIM), lambda i: (0, 0))
    kv_out = pl.BlockSpec((tr, KV_WIDTH), lambda i: (i, 0))
    return pl.pallas_call(
        _qkv_prep_kernel,
        grid=(rows // tr,),
        in_specs=[pl.BlockSpec((tr, ATTN_WIDTH), lambda i: (i, 0)),
                  pl.BlockSpec((tr, 2 * KV_WIDTH), lambda i: (i, Q_END // (2 * KV_WIDTH))),
                  vec, vec, tab, tab, tab],
        out_specs=[pl.BlockSpec((tr, ATTN_WIDTH), lambda i: (i, 0)), kv_out, kv_out],
        out_shape=[jax.ShapeDtypeStruct((rows, ATTN_WIDTH), BF16),
                   jax.ShapeDtypeStruct((rows, KV_WIDTH), BF16),
                   jax.ShapeDtypeStruct((rows, KV_WIDTH), BF16)],
        compiler_params=_cparams(1),
        name="qkv_prep",
    )(p, p, q_g, k_g, cos, sin_a, sin_b)


def _kv_prep_ctx_kernel(kv_ref, kg_ref, ko_ref, vo_ref):
    for h in range(N_KV_HEADS):
        sl = slice(h * HEAD_DIM, (h + 1) * HEAD_DIM)
        ko_ref[:, sl] = _head_prep(kv_ref[:, sl], kg_ref[...], None, None, None, False).astype(ko_ref.dtype)
    vo_ref[...] = kv_ref[:, KV_WIDTH:].astype(vo_ref.dtype)


def _kv_prep_ctx(pc_kv, k_g):
    tr = ROW_TILE
    rows = pc_kv.shape[0]
    kv_out = pl.BlockSpec((tr, KV_WIDTH), lambda i: (i, 0))
    return pl.pallas_call(
        _kv_prep_ctx_kernel,
        grid=(rows // tr,),
        in_specs=[pl.BlockSpec((tr, 2 * KV_WIDTH), lambda i: (i, 0)),
                  pl.BlockSpec((1, HEAD_DIM), lambda i: (0, 0))],
        out_specs=[kv_out, kv_out],
        out_shape=[jax.ShapeDtypeStruct((rows, KV_WIDTH), BF16)] * 2,
        compiler_params=_cparams(1),
        name="kv_prep_ctx",
    )(pc_kv, k_g)


def _attn_kernel(q_ref, k_ref, v_ref, o_ref):
    k = k_ref[...]
    v = v_ref[...]
    for g in range(Q_PER_KV):
        sl = slice(g * HEAD_DIM, (g + 1) * HEAD_DIM)
        s = lax.dot_general(q_ref[:, sl], k, (((1,), (1,)), ((), ())), preferred_element_type=F32)
        m = jnp.max(s, axis=-1, keepdims=True)
        p = jnp.exp2(s - m)
        l = jnp.sum(p, axis=-1, keepdims=True)
        o = jnp.dot(p.astype(BF16), v, preferred_element_type=F32)
        o_ref[:, sl] = (o / l).astype(o_ref.dtype)


def _attention(q, k_all, v_all, n_lat):
    tq = 512
    b, t, _ = k_all.shape
    nq = n_lat // tq
    gw = Q_PER_KV * HEAD_DIM
    return pl.pallas_call(
        _attn_kernel,
        grid=(b, N_KV_HEADS, nq),
        in_specs=[pl.BlockSpec((tq, gw), lambda bi, h, i: (bi * nq + i, h)),
                  pl.BlockSpec((None, t, HEAD_DIM), lambda bi, h, i: (bi, 0, h)),
                  pl.BlockSpec((None, t, HEAD_DIM), lambda bi, h, i: (bi, 0, h))],
        out_specs=pl.BlockSpec((tq, gw), lambda bi, h, i: (bi * nq + i, h)),
        out_shape=jax.ShapeDtypeStruct(q.shape, BF16),
        compiler_params=_cparams(3),
        name="attention",
    )(q, k_all, v_all)


def _filter_kernel(z_ref, w1_ref, b1_ref, w2_ref, b2_ref, w3_ref, b3_ref, fr_ref, wo_ref, t_ref, ad_ref, o_ref,
                   a_ref):
    j = pl.program_id(0)
    hi = lax.Precision.HIGHEST

    @pl.when(j == 0)
    def _():
        fr = fr_ref[...]
        a = jnp.sin(fr * (jnp.dot(z_ref[...], w1_ref[...], precision=hi, preferred_element_type=F32)
                          + b1_ref[...]))
        a = jnp.sin(fr * (jnp.dot(a, w2_ref[...], precision=hi, preferred_element_type=F32) + b2_ref[...]))
        a_ref[...] = jnp.sin(fr * (jnp.dot(a, w3_ref[...], precision=hi, preferred_element_type=F32)
                                   + b3_ref[...]))

    h = jnp.dot(a_ref[...], wo_ref[...], precision=hi, preferred_element_type=F32)
    h = h * jnp.exp(-t_ref[...] * ad_ref[...])
    backward = (j // 2) % 2 == 1
    row = lax.broadcasted_iota(jnp.int32, h.shape, 0)
    o_ref[...] = jnp.where(jnp.logical_and(backward, row == 0), 0.0, h)


def _pad2(a, rows, cols):
    return jnp.pad(a, ((0, rows - a.shape[0]), (0, cols - a.shape[1])))


def _hyena_filters(n, w1, b1, w2, b2, w3, b3, freq, w_out):
    t = jnp.linspace(0.0, 1.0, n, dtype=F32)[:, None]
    bands = (FILTER_EMB - 1) // 2
    w = 2.0 * math.pi * jnp.arange(n, dtype=F32)[:, None] / n
    f = jnp.linspace(1e-4, bands - 1, bands, dtype=F32)[None, :]
    z = jnp.concatenate([t, jnp.cos(f * w), -jnp.sin(f * w)], axis=-1)
    max_decay = math.log(DECAY_TARGET) / FAST_DECAY_PCT
    min_decay = math.log(DECAY_TARGET) / SLOW_DECAY_PCT
    ad = jnp.abs(jnp.linspace(min_decay, max_decay, HYENA_WIDTH, dtype=F32))[None, :]
    p = FILT_PAD
    tn = 1024
    n_out = w_out.shape[1]
    full = lambda shape: pl.BlockSpec(shape, lambda j: (0, 0))
    return pl.pallas_call(
        _filter_kernel,
        grid=(n_out // tn,),
        in_specs=[full((n, p)), full((p, p)), full((1, p)), full((p, p)), full((1, p)), full((p, p)),
                  full((1, p)), full((1, p)),
                  pl.BlockSpec((p, tn), lambda j: (0, j)),
                  full((n, 1)),
                  pl.BlockSpec((1, tn), lambda j: (0, j % (HYENA_WIDTH // tn)))],
        out_specs=pl.BlockSpec((n, tn), lambda j: (0, j)),
        out_shape=jax.ShapeDtypeStruct((n, n_out), F32),
        scratch_shapes=[pltpu.VMEM((n, p), F32)],
        compiler_params=_cparams(1),
        name="hyena_filters",
    )(_pad2(z, n, p), _pad2(w1, p, p), _pad2(b1[None, :], 1, p), _pad2(w2, p, p), _pad2(b2[None, :], 1, p),
      _pad2(w3, p, p), _pad2(b3[None, :], 1, p), _pad2(freq[None, :], 1, p), _pad2(w_out, p, n_out), t, ad)


def _dft_matrix_kernel(ca_ref, sa_ref, cb_ref, sb_ref, alt_ref, cs_ref, fre_ref, fim_ref, gre_ref, gim_ref):
    i = pl.program_id(0)
    ca, sa = ca_ref[...], sa_ref[...]
    cb, sb = cb_ref[...], sb_ref[...]
    cos = ca * cb - sa * sb
    sin = sa * cb + ca * sb
    cs = cs_ref[...]
    row = lax.broadcasted_iota(jnp.int32, cos.shape, 0)
    col = lax.broadcasted_iota(jnp.int32, cos.shape, 1)
    nyquist_row = jnp.logical_and(i == 0, row == 0)
    alt_rows = jnp.where(row % 2 == 0, 1.0, -1.0)
    fre_ref[...] = cos.astype(fre_ref.dtype)
    fim_ref[...] = jnp.where(nyquist_row, alt_ref[...], -sin).astype(fim_ref.dtype)
    gre_ref[...] = (cos * cs).astype(gre_ref.dtype)
    gim_ref[...] = (jnp.where(col == 0, alt_rows, -sin) * cs).astype(gim_ref.dtype)


def _dft_matrices(n):
    big = 2 * n
    rb = DFT_ROWS
    t = jnp.arange(n, dtype=jnp.int32)[None, :]
    unit = 2.0 * math.pi / big
    coarse = ((rb * jnp.arange(n // rb, dtype=jnp.int32)[:, None] * t) % big).astype(F32) * unit
    fine = ((jnp.arange(rb, dtype=jnp.int32)[:, None] * t) % big).astype(F32) * unit
    alt = jnp.where(t % 2 == 0, 1.0, -1.0).astype(F32)
    cs = jnp.where(t == 0, 1.0 / big, 2.0 / big).astype(F32)
    coarse_spec = pl.BlockSpec((None, 1, n), lambda i: (i, 0, 0))
    whole = lambda r: pl.BlockSpec((r, n), lambda i: (0, 0))
    out_spec = pl.BlockSpec((rb, n), lambda i: (i, 0))
    return pl.pallas_call(
        _dft_matrix_kernel,
        grid=(n // rb,),
        in_specs=[coarse_spec, coarse_spec, whole(rb), whole(rb), whole(1), whole(1)],
        out_specs=[out_spec] * 4,
        out_shape=[jax.ShapeDtypeStruct((n, n), BF16)] * 4,
        compiler_params=_cparams(1),
        name="dft_matrices",
    )(jnp.cos(coarse)[:, None, :], jnp.sin(coarse)[:, None, :], jnp.cos(fine), jnp.sin(fine), alt, cs)


def _spectrum_kernel(fre_ref, fim_ref, hf_ref, hb_ref, kre_ref, kim_ref):
    i = pl.program_id(0)
    hf = hf_ref[...].astype(BF16)
    hb = hb_ref[...].astype(BF16)
    fre = fre_ref[...]
    fim = fim_ref[...]
    kre_ref[...] = (jnp.dot(fre, hf, preferred_element_type=F32) + jnp.dot(fre, hb, preferred_element_type=F32))
    d1 = jnp.dot(fim, hf, preferred_element_type=F32)
    d2 = jnp.dot(fim, hb, preferred_element_type=F32)
    row = lax.broadcasted_iota(jnp.int32, d1.shape, 0)
    nyq = jnp.logical_and(i == 0, row == 0)
    kim_ref[...] = jnp.where(nyq, d1 + d2, d1 - d2)


def _filter_spectra(h_time, f_re, f_im):
    n = h_time.shape[0]
    tf, tc = 1024, 512
    per = HYENA_WIDTH // tc
    return pl.pallas_call(
        _spectrum_kernel,
        grid=(n // tf, 2 * per),
        in_specs=[pl.BlockSpec((tf, n), lambda i, j: (i, 0)),
                  pl.BlockSpec((tf, n), lambda i, j: (i, 0)),
                  pl.BlockSpec((n, tc), lambda i, j: (0, (j // per) * 2 * per + j % per)),
                  pl.BlockSpec((n, tc), lambda i, j: (0, (j // per) * 2 * per + per + j % per))],
        out_specs=[pl.BlockSpec((tf, tc), lambda i, j: (i, j))] * 2,
        out_shape=[jax.ShapeDtypeStruct((n, 2 * HYENA_WIDTH), F32)] * 2,
        compiler_params=_cparams(2),
        name="filter_spectra",
    )(f_re, f_im, h_time, h_time)


def _short_conv_kernel(p_ref, w_ref, b_ref, u_ref, v16_ref):
    j = pl.program_id(1)
    x = p_ref[...]
    n = x.shape[0]
    row = lax.broadcasted_iota(jnp.int32, x.shape, 0)
    prev = jnp.where(row == 0, 0.0, pltpu.roll(x, 1, axis=0))
    nxt = jnp.where(row == n - 1, 0.0, pltpu.roll(x, n - 1, axis=0))
    u = prev * w_ref[0:1, :] + x * w_ref[1:2, :] + nxt * w_ref[2:3, :] + b_ref[...]
    u_ref[...] = u

    @pl.when(j < HYENA_WIDTH // u.shape[1])
    def _():
        v16_ref[...] = u.astype(v16_ref.dtype)


def _short_conv(p3, w, b):
    bsz, n, _ = p3.shape
    tc = 512
    nj = 3 * HYENA_WIDTH // tc
    nv = HYENA_WIDTH // tc
    off = V_END // tc
    return pl.pallas_call(
        _short_conv_kernel,
        grid=(bsz, nj),
        in_specs=[pl.BlockSpec((None, n, tc), lambda bi, j: (bi, 0, off + j)),
                  pl.BlockSpec((3, tc), lambda bi, j: (0, j)),
                  pl.BlockSpec((1, tc), lambda bi, j: (0, j))],
        out_specs=[pl.BlockSpec((None, n, tc), lambda bi, j: (bi, 0, j)),
                   pl.BlockSpec((None, n, tc), lambda bi, j: (bi, 0, jnp.minimum(j, nv - 1)))],
        out_shape=[jax.ShapeDtypeStruct((bsz, n, 3 * HYENA_WIDTH), F32),
                   jax.ShapeDtypeStruct((bsz, n, HYENA_WIDTH), BF16)],
        compiler_params=_cparams(2),
        name="short_conv",
    )(p3, w, b)


def _dft_fwd_kernel(fre_ref, fim_ref, u_ref, kre_ref, kim_ref, yre_ref, yim_ref):
    f = pl.program_id(2)
    u = u_ref[...]
    ure = jnp.dot(fre_ref[...], u, preferred_element_type=F32)
    uim = jnp.dot(fim_ref[...], u, preferred_element_type=F32)
    kre = kre_ref[...]
    kim = kim_ref[...]
    row = lax.broadcasted_iota(jnp.int32, ure.shape, 0)
    real_pair = jnp.logical_and(f == 0, row == 0)
    yre_ref[...] = jnp.where(real_pair, ure * kre, ure * kre - uim * kim).astype(yre_ref.dtype)
    yim_ref[...] = jnp.where(real_pair, uim * kim, ure * kim + uim * kre).astype(yim_ref.dtype)


def _dft_forward(u16, f_re, f_im, k_re, k_im, order):
    bsz, n, ch = u16.shape
    tf, tc = 1024, 512
    nc = ch // tc
    return pl.pallas_call(
        _dft_fwd_kernel,
        grid=(bsz, nc, n // tf),
        in_specs=[pl.BlockSpec((tf, n), lambda bi, c, f: (f, 0)),
                  pl.BlockSpec((tf, n), lambda bi, c, f: (f, 0)),
                  pl.BlockSpec((None, n, tc), lambda bi, c, f: (bi, 0, c)),
                  pl.BlockSpec((tf, tc), lambda bi, c, f: (f, order * nc + c)),
                  pl.BlockSpec((tf, tc), lambda bi, c, f: (f, order * nc + c))],
        out_specs=[pl.BlockSpec((None, tf, tc), lambda bi, c, f: (bi, f, c))] * 2,
        out_shape=[jax.ShapeDtypeStruct((bsz, n, ch), BF16)] * 2,
        compiler_params=_cparams(3),
        name="dft_forward",
    )(f_re, f_im, u16, k_re, k_im)


def _dft_inv_kernel(gre_ref, gim_ref, yre_ref, yim_ref, u_ref, gate_ref, bias_ref, *o_refs):
    y = (jnp.dot(gre_ref[...], yre_ref[...], preferred_element_type=F32)
         + jnp.dot(gim_ref[...], yim_ref[...], preferred_element_type=F32))
    z = gate_ref[...] * (y + u_ref[...] * bias_ref[...])
    for o in o_refs:
        o[...] = z.astype(o.dtype)


def _dft_inverse(y_re, y_im, g_re, g_im, u_arr, u_blk0, gate_arr, gate_blk0, bias, out_dtypes):
    bsz, n, ch = y_re.shape
    tt, tc = 1024, 512
    nc = ch // tc
    return pl.pallas_call(
        _dft_inv_kernel,
        grid=(bsz, nc, n // tt),
        in_specs=[pl.BlockSpec((tt, n), lambda bi, c, t: (t, 0)),
                  pl.BlockSpec((tt, n), lambda bi, c, t: (t, 0)),
                  pl.BlockSpec((None, n, tc), lambda bi, c, t: (bi, 0, c)),
                  pl.BlockSpec((None, n, tc), lambda bi, c, t: (bi, 0, c)),
                  pl.BlockSpec((None, tt, tc), lambda bi, c, t: (bi, t, u_blk0 * nc + c)),
                  pl.BlockSpec((None, tt, tc), lambda bi, c, t: (bi, t, gate_blk0 * nc + c)),
                  pl.BlockSpec((1, tc), lambda bi, c, t: (0, c))],
        out_specs=[pl.BlockSpec((None, tt, tc), lambda bi, c, t: (bi, t, c))] * len(out_dtypes),
        out_shape=[jax.ShapeDtypeStruct((bsz, n, ch), dt) for dt in out_dtypes],
        compiler_params=_cparams(3),
        name="dft_inverse",
    )(g_re, g_im, y_re, y_im, u_arr, gate_arr, bias)


def _ffn(h, w_gate, w_up, w_down, lead, rows):
    n_tiles = pl.cdiv(D_FF, MM_TN)
    act = _matmul_wres([h], [w_gate, w_up], [(0, 0), (0, 1)], _ep_swiglu, D_FF, [BF16], tm=MM_TM, tn=MM_TN,
                       m_tiles=rows // MM_TM, n_tiles=n_tiles, k=D_MODEL, ch=MM_CHUNK, lead=[lead, lead],
                       last_width=D_FF - (n_tiles - 1) * MM_TN, name="ffn_gate_up")[0]
    return _matmul_wres([act], [w_down], [(0, 0)], _ep_store, D_MODEL, [F32], tm=DOWN_TM, tn=MM_TN,
                        m_tiles=rows // DOWN_TM, n_tiles=D_MODEL // MM_TN, k=D_FF, ch=DOWN_CHUNK, lead=[lead],
                        name="ffn_down")[0]


def kernel(x, c, ctx, c_ctx, w_ada, b_ada, pre_g, post_g, ffn_w_gate, ffn_w_up, ffn_w_down, w_in, q_norm_g,
           k_norm_g, short_w, short_b, filt_w1, filt_b1, filt_w2, filt_b2, filt_w3, filt_b3, filt_freq,
           filt_w_out, hyena_bias, w_br_attn, w_br_hyena, w_out):
    bsz, n_lat, d = x.shape
    n_ctx = ctx.shape[1]
    rows_x, rows_c = bsz * n_lat, bsz * n_ctx
    x2d = x.reshape(rows_x, d)
    c2d = ctx.reshape(rows_c, d)
    l = 0

    cond = jnp.concatenate([c, c_ctx[None, :], jnp.zeros((3, d), F32)], axis=0)
    mods = _ada(cond, w_ada, b_ada[l][None, :], l)[:5].reshape(5, N_MOD, d)
    pre = [pre_g[l, s][None, :] for s in range(3)]
    post = [post_g[l, s][None, :] for s in range(3)]

    h0 = _modulate_first(x2d, c2d, mods, pre[0], shift_i=0, scale_i=1)
    f1 = _ffn(h0, ffn_w_gate, ffn_w_up, ffn_w_down, (l, 0), rows_x + rows_c)
    x1, h1 = _residual(f1, x2d, c2d, mods, post[0], pre[1], gate_i=2, weight=MACARON_W, shift_i=3, scale_i=4)

    tm, tn = MM_TM, MM_TN
    p = _matmul_wres([h1], [w_in], [(0, 0)], _ep_store, IN_COLS, [F32], tm=tm, tn=tn, m_tiles=rows_x // tm,
                     n_tiles=IN_COLS // tn, k=d, ch=MM_CHUNK, lead=[(l,)], name="in_proj")[0]
    pc_kv = _matmul_wres([h1], [w_in], [(0, 0)], _ep_store, 2 * KV_WIDTH, [F32], tm=CTX_TM, tn=tn,
                         m_tiles=rows_c // CTX_TM, n_tiles=2 * KV_WIDTH // tn, k=d, ch=MM_CHUNK, lead=[(l,)],
                         col0=Q_END, a_blk0=rows_x // CTX_TM, name="in_proj_ctx")[0]

    tables = _rope_tables(n_lat)
    q16, k16, v16 = _qkv_prep(p, q_norm_g[l][None, :], k_norm_g[l][None, :], tables)
    kc16, vc16 = _kv_prep_ctx(pc_kv, k_norm_g[l][None, :])
    k_all = jnp.concatenate([kc16.reshape(bsz, n_ctx, KV_WIDTH), k16.reshape(bsz, n_lat, KV_WIDTH)], axis=1)
    v_all = jnp.concatenate([vc16.reshape(bsz, n_ctx, KV_WIDTH), v16.reshape(bsz, n_lat, KV_WIDTH)], axis=1)
    attn_o = _attention(q16, k_all, v_all, n_lat)

    f_re, f_im, g_re, g_im = _dft_matrices(n_lat)
    h_time = _hyena_filters(n_lat, filt_w1[l], filt_b1[l], filt_w2[l], filt_b2[l], filt_w3[l], filt_b3[l],
                            filt_freq[l], filt_w_out[l])
    k_re, k_im = _filter_spectra(h_time, f_re, f_im)
    p3 = p.reshape(bsz, n_lat, IN_COLS)
    u, v16h = _short_conv(p3, short_w[l], short_b[l][None, :])
    y_re, y_im = _dft_forward(v16h, f_re, f_im, k_re, k_im, order=0)
    z, z16 = _dft_inverse(y_re, y_im, g_re, g_im, u, 0, u, 1, hyena_bias[l, 0][None, :], [F32, BF16])
    y_re, y_im = _dft_forward(z16, f_re, f_im, k_re, k_im, order=1)
    hy_o = _dft_inverse(y_re, y_im, g_re, g_im, z, 0, u, 2, hyena_bias[l, 1][None, :], [BF16])[0]

    gate_spec = lambda off: pl.BlockSpec((tm, tn), lambda j, i: (i, off + j))
    merged = _matmul_wres([attn_o, hy_o.reshape(rows_x, HYENA_WIDTH)], [w_br_attn, w_br_hyena], [(0, 0), (1, 1)],
                          _ep_merge, d, [BF16], tm=tm, tn=tn, m_tiles=rows_x // tm, n_tiles=d // tn,
                          k=ATTN_WIDTH, ch=MM_CHUNK, lead=[(l,), (l,)], extras=(p, p),
                          extra_specs=(gate_spec(HY_END // tn), gate_spec((HY_END + d) // tn)), name="merge")[0]
    out = _matmul_wres([merged], [w_out], [(0, 0)], _ep_store, d, [F32], tm=tm, tn=tn, m_tiles=rows_x // tm,
                       n_tiles=d // tn, k=d, ch=MM_CHUNK, lead=[(l,)], name="out_proj")[0]
    x2, h2 = _residual(out, x1, None, mods, post[1], pre[2], gate_i=5, weight=1.0, shift_i=6, scale_i=7)

    f2 = _ffn(h2, ffn_w_gate, ffn_w_up, ffn_w_down, (l, 1), rows_x)
    x3 = _residual(f2, x2, None, mods, post[2], None, gate_i=8, weight=MACARON_W)[0]
    return x3.reshape(bsz, n_lat, d)
```

```python
import functools
import math
from typing import NamedTuple

import jax
import jax.numpy as jnp
from jax import lax
from jax.experimental import pallas as pl
from jax.experimental.pallas import tpu as pltpu

F32 = jnp.float32
BF16 = jnp.bfloat16

D_MODEL = 4096
GRID_W = 64
HEAD_DIM = 128
N_Q_HEADS = 16
N_KV_HEADS = 4
Q_PER_KV = N_Q_HEADS // N_KV_HEADS
ATTN_WIDTH = N_Q_HEADS * HEAD_DIM
KV_WIDTH = N_KV_HEADS * HEAD_DIM
ROPE_THETA = 10000.0
AXIS_DIM = HEAD_DIM // 2
HYENA_WIDTH = D_MODEL // 2
FILTER_EMB = 33
FILTER_HIDDEN = 64
DECAY_TARGET = 1e-2
FAST_DECAY_PCT = 0.3
SLOW_DECAY_PCT = 1.5
D_FF = 11008
MACARON_W = 0.5
N_MOD = 9
NORM_EPS = 1e-6
Q_END = ATTN_WIDTH
K_END = Q_END + KV_WIDTH
V_END = K_END + KV_WIDTH
HY_END = V_END + 3 * HYENA_WIDTH
IN_COLS = HY_END + 2 * D_MODEL
SCORE_SCALE_LOG2 = math.log2(math.e) / math.sqrt(HEAD_DIM)

V7X_LANES = 128
BF16_SUBLANES = 16
V7X_VMEM_LIMIT_BYTES = 56 * 1024 * 1024

MM_TM = 1024
MM_TN = 512
MM_CHUNK = 1024
IN_TN = 1024
IN_CHUNK = 512
CTX_TM = 256
DOWN_TM = 512
DOWN_CHUNK = D_FF // 16
ROW_TILE = 256
ADA_SIDE_TILE = 256
DFT_ROWS = 64
FILT_PAD = V7X_LANES


def _cparams(n_axes):
    return pltpu.CompilerParams(dimension_semantics=("arbitrary",) * n_axes,
                                vmem_limit_bytes=V7X_VMEM_LIMIT_BYTES)


def _rms(x):
    return x * lax.rsqrt(jnp.mean(x * x, axis=-1, keepdims=True) + NORM_EPS)


def _ada_kernel(c_ref, w_ref, b_ref, o_ref):
    c = c_ref[...]
    s = (c * jax.nn.sigmoid(c)).astype(BF16)
    o_ref[...] = jnp.dot(s, w_ref[...].astype(BF16), preferred_element_type=F32) + b_ref[...]


def _ada(cond, w, b, layer, n):
    tn = 512
    return pl.pallas_call(
        _ada_kernel,
        grid=(n // tn,),
        in_specs=[pl.BlockSpec((8, D_MODEL), lambda j: (0, 0)),
                  pl.BlockSpec((None, D_MODEL, tn), lambda j: (layer, 0, j)),
                  pl.BlockSpec((1, tn), lambda j: (0, j))],
        out_specs=pl.BlockSpec((8, tn), lambda j: (0, j)),
        out_shape=jax.ShapeDtypeStruct((8, n), F32),
        compiler_params=_cparams(1),
        name="ada",
    )(cond, w, b)


def _mod_index(i, rows_per_batch_tiles):
    return jnp.minimum(i // rows_per_batch_tiles, 4)


def _modulate_kernel(x_ref, c_ref, m_ref, g_ref, o_ref, *, nx, shift_i, scale_i):
    i = pl.program_id(0)

    def body(src):
        y = _rms(src[...]) * g_ref[...]
        o_ref[...] = (y * (1.0 + m_ref[0, scale_i:scale_i + 1, :])
                      + m_ref[0, shift_i:shift_i + 1, :]).astype(o_ref.dtype)

    pl.when(i < nx)(lambda: body(x_ref))
    pl.when(i >= nx)(lambda: body(c_ref))


def _modulate_first(x2d, c2d, mods, g, shift_i, scale_i):
    tr = ROW_TILE
    nx, nc = x2d.shape[0] // tr, c2d.shape[0] // tr
    per_batch = (x2d.shape[0] // 4) // tr
    return pl.pallas_call(
        functools.partial(_modulate_kernel, nx=nx, shift_i=shift_i, scale_i=scale_i),
        grid=(nx + nc,),
        in_specs=[pl.BlockSpec((tr, D_MODEL), lambda i: (jnp.minimum(i, nx - 1), 0)),
                  pl.BlockSpec((tr, D_MODEL), lambda i: (jnp.maximum(i - nx, 0), 0)),
                  pl.BlockSpec((1, mods.shape[1], D_MODEL), lambda i: (_mod_index(i, per_batch), 0, 0)),
                  pl.BlockSpec((1, D_MODEL), lambda i: (0, 0))],
        out_specs=pl.BlockSpec((tr, D_MODEL), lambda i: (i, 0)),
        out_shape=jax.ShapeDtypeStruct(((nx + nc) * tr, D_MODEL), BF16),
        compiler_params=_cparams(1),
        name="modulate_first",
    )(x2d, c2d, mods, g)


def _resid_kernel(*refs, nx, two_source, gate_i, weight, shift_i, scale_i, with_next):
    it = iter(refs)
    f_ref = next(it)
    x_ref = next(it)
    c_ref = next(it) if two_source else None
    m_ref = next(it)
    pg_ref = next(it)
    ng_ref = next(it) if with_next else None
    xo_ref = next(it)
    ho_ref = next(it) if with_next else None
    i = pl.program_id(0)

    def body(src):
        y = _rms(f_ref[...]) * pg_ref[...]
        xn = src[...] + (weight * m_ref[0, gate_i:gate_i + 1, :]) * y
        xo_ref[...] = xn
        if with_next:
            h = _rms(xn) * ng_ref[...]
            ho_ref[...] = (h * (1.0 + m_ref[0, scale_i:scale_i + 1, :])
                           + m_ref[0, shift_i:shift_i + 1, :]).astype(ho_ref.dtype)

    if two_source:
        pl.when(i < nx)(lambda: body(x_ref))
        pl.when(i >= nx)(lambda: body(c_ref))
    else:
        body(x_ref)


def _residual(f, x2d, c2d, mods, post_g, next_g, *, gate_i, weight, shift_i=0, scale_i=0):
    tr = ROW_TILE
    rows = f.shape[0]
    n = rows // tr
    two_source = c2d is not None
    with_next = next_g is not None
    nx = x2d.shape[0] // tr if two_source else n
    per_batch = (8192 // 4) // tr
    row = pl.BlockSpec((tr, D_MODEL), lambda i: (i, 0))
    vec = pl.BlockSpec((1, D_MODEL), lambda i: (0, 0))
    in_specs = [row]
    args = [f]
    if two_source:
        in_specs += [pl.BlockSpec((tr, D_MODEL), lambda i: (jnp.minimum(i, nx - 1), 0)),
                     pl.BlockSpec((tr, D_MODEL), lambda i: (jnp.maximum(i - nx, 0), 0))]
        args += [x2d, c2d]
    else:
        in_specs += [row]
        args += [x2d]
    in_specs += [pl.BlockSpec((1, N_MOD, D_MODEL), lambda i: (_mod_index(i, per_batch), 0, 0)), vec]
    args += [mods, post_g]
    out_specs = [row]
    out_shape = [jax.ShapeDtypeStruct((rows, D_MODEL), F32)]
    if with_next:
        in_specs += [vec]
        args += [next_g]
        out_specs += [row]
        out_shape += [jax.ShapeDtypeStruct((rows, D_MODEL), BF16)]
    return pl.pallas_call(
        functools.partial(_resid_kernel, nx=nx, two_source=two_source, gate_i=gate_i, weight=weight,
                          shift_i=shift_i, scale_i=scale_i, with_next=with_next),
        grid=(n,),
        in_specs=in_specs,
        out_specs=out_specs,
        out_shape=out_shape,
        compiler_params=_cparams(1),
        name="residual",
    )(*args)


class _WeightPlan(NamedTuple):
    k: int
    ch: int
    tn: int
    n_tiles: int
    m_tiles: int
    col0: int
    last_width: int
    lead: tuple


def _wres_kernel(*refs, n_a, n_w, pairs, n_extra, n_out, epilogue, plan, side):
    it = iter(refs)
    a_refs = [next(it) for _ in range(n_a)]
    w_refs = [next(it) for _ in range(n_w)]
    e_refs = [next(it) for _ in range(n_extra)]
    if side is not None:
        side_c_ref, side_w_ref, side_b_ref = next(it), next(it), next(it)
    o_refs = [next(it) for _ in range(n_out)]
    if side is not None:
        side_o_ref = next(it)
    wbf, stage, sem = next(it), next(it), next(it)
    if side is not None:
        side_stage, side_sem = next(it), next(it)
    j = pl.program_id(0)
    i = pl.program_id(1)
    slot = j % 2
    per_w = plan.k // plan.ch
    n_chunks = n_w * per_w
    ragged = plan.last_width != plan.tn

    def chunk_copy(tile, c, width):
        w, r = divmod(c, per_w)
        col = plan.col0 + tile * plan.tn
        if not isinstance(col, int):
            col = pl.multiple_of(col, V7X_LANES)
        src = w_refs[w].at[(*plan.lead[w], pl.ds(r * plan.ch, plan.ch), pl.ds(col, width))]
        return pltpu.make_async_copy(src, stage.at[c % 2, :, pl.ds(0, width)], sem.at[c % 2])

    def width_variants(tile):
        if not ragged:
            return [(plan.tn, tile < plan.n_tiles)]
        return [(plan.tn, tile < plan.n_tiles - 1), (plan.last_width, tile == plan.n_tiles - 1)]

    @pl.when(jnp.logical_and(j == 0, i == 0))
    def _():
        chunk_copy(0, 0, plan.tn).start()
        for c in range(n_chunks):
            w, r = divmod(c, per_w)
            chunk_copy(0, c, plan.tn).wait()
            if c + 1 < n_chunks:
                chunk_copy(0, c + 1, plan.tn).start()
            wbf[0, w, pl.ds(r * plan.ch, plan.ch), :] = stage[c % 2].astype(BF16)
        second = plan.last_width if plan.n_tiles == 2 else plan.tn
        chunk_copy(1, 0, second).start()

    for width, cond in width_variants(j + 1):
        for c in range(n_chunks):
            @pl.when(jnp.logical_and(cond, i == c))
            def _(width=width, c=c):
                chunk_copy(j + 1, c, width).wait()
                if c + 1 < n_chunks:
                    chunk_copy(j + 1, c + 1, width).start()

    for width, cond in width_variants(j + 2):
        @pl.when(jnp.logical_and(cond, i == plan.m_tiles - 1))
        def _(width=width):
            chunk_copy(j + 2, 0, width).start()

    if side is not None:
        _side_stream_dma(side, side_w_ref, side_stage, side_sem, j * plan.m_tiles + i)

    def compute(width):
        c_now = jnp.minimum(i, n_chunks - 1)
        w_now = c_now // per_w
        row_now = pl.multiple_of((c_now % per_w) * plan.ch, BF16_SUBLANES)
        wbf[1 - slot, w_now, pl.ds(row_now, plan.ch), :] = stage[c_now % 2].astype(BF16)
        if side is not None:
            _side_stream_compute(side, side_c_ref, side_b_ref, side_o_ref, side_stage, j * plan.m_tiles + i)
        prods = [jnp.dot(a_refs[ai][...], wbf[slot, wi, :, pl.ds(0, width)], preferred_element_type=F32)
                 for ai, wi in pairs]
        epilogue(prods, e_refs, o_refs, width)

    if ragged:
        pl.when(j < plan.n_tiles - 1)(lambda: compute(plan.tn))
        pl.when(j == plan.n_tiles - 1)(lambda: compute(plan.last_width))
    else:
        compute(plan.tn)


class _SidePlan(NamedTuple):
    lead: tuple
    k: int
    ts: int
    n_tiles: int
    col0: int


def _side_stream_dma(side, w_ref, stage, sem, g):
    def copy(tile, slot):
        col = pl.multiple_of(side.col0 + tile * side.ts, V7X_LANES)
        return pltpu.make_async_copy(w_ref.at[(*side.lead, slice(None), pl.ds(col, side.ts))],
                                     stage.at[slot], sem.at[slot])

    pl.when(g == 0)(lambda: copy(g, 0).start())
    for slot in range(2):
        pl.when(jnp.logical_and(g < side.n_tiles, g % 2 == slot))(lambda slot=slot: copy(g, slot).wait())
    for slot in range(2):
        pl.when(jnp.logical_and(g + 1 < side.n_tiles, (g + 1) % 2 == slot))(
            lambda slot=slot: copy(g + 1, slot).start())


def _side_stream_compute(side, c_ref, b_ref, o_ref, stage, g):
    t = jnp.minimum(g, side.n_tiles - 1)
    c = c_ref[...]
    s = (c * jax.nn.sigmoid(c)).astype(BF16)
    o_ref[...] = jnp.dot(s, stage[t % 2].astype(BF16), preferred_element_type=F32) + b_ref[...]


def _matmul_wres(a_list, w_list, pairs, epilogue, out_cols, out_dtypes, *, tm, tn, m_tiles, n_tiles, k, ch,
                 lead, col0=0, last_width=None, a_blk0=0, extras=(), extra_specs=(), side=None, side_args=(),
                 name="matmul"):
    plan = _WeightPlan(k=k, ch=ch, tn=tn, n_tiles=n_tiles, m_tiles=m_tiles, col0=col0,
                       last_width=last_width or tn, lead=tuple(lead))
    n_chunks = len(w_list) * (k // ch)
    assert n_chunks <= m_tiles and n_chunks % 2 == 0 and n_tiles >= 2 and k % ch == 0
    in_specs = [pl.BlockSpec((tm, k), lambda j, i: (i + a_blk0, 0)) for _ in a_list]
    in_specs += [pl.BlockSpec(memory_space=pl.ANY) for _ in w_list]
    in_specs += list(extra_specs)
    out_specs = [pl.BlockSpec((tm, tn), lambda j, i: (i, j)) for _ in out_dtypes]
    out_shape = [jax.ShapeDtypeStruct((m_tiles * tm, out_cols), dt) for dt in out_dtypes]
    scratch = [pltpu.VMEM((2, len(w_list), k, tn), BF16), pltpu.VMEM((2, ch, tn), F32),
               pltpu.SemaphoreType.DMA((2,))]
    if side is not None:
        assert side.n_tiles <= n_tiles * m_tiles
        side_tile = lambda j, i: (0, jnp.minimum(j * m_tiles + i, side.n_tiles - 1))
        in_specs += [pl.BlockSpec((8, side.k), lambda j, i: (0, 0)), pl.BlockSpec(memory_space=pl.ANY),
                     pl.BlockSpec((1, side.ts), side_tile)]
        out_specs += [pl.BlockSpec((8, side.ts), side_tile)]
        out_shape += [jax.ShapeDtypeStruct((8, side.n_tiles * side.ts), F32)]
        scratch += [pltpu.VMEM((2, side.k, side.ts), F32), pltpu.SemaphoreType.DMA((2,))]
    return pl.pallas_call(
        functools.partial(_wres_kernel, n_a=len(a_list), n_w=len(w_list), pairs=tuple(pairs),
                          n_extra=len(extras), n_out=len(out_dtypes), epilogue=epilogue, plan=plan, side=side),
        grid=(n_tiles, m_tiles),
        in_specs=in_specs,
        out_specs=out_specs,
        out_shape=out_shape,
        scratch_shapes=scratch,
        compiler_params=_cparams(2),
        name=name,
    )(*a_list, *w_list, *extras, *side_args)


def _ep_store(accs, e_refs, o_refs, width):
    o_refs[0][:, pl.ds(0, width)] = accs[0].astype(o_refs[0].dtype)


def _ep_swiglu(accs, e_refs, o_refs, width):
    g, u = accs
    o_refs[0][:, pl.ds(0, width)] = (g * jax.nn.sigmoid(g) * u).astype(o_refs[0].dtype)


def _ep_merge(accs, e_refs, o_refs, width):
    a, h = accs
    ga, gh = e_refs
    o_refs[0][...] = (jax.nn.sigmoid(ga[...]) * a + jax.nn.sigmoid(gh[...]) * h).astype(o_refs[0].dtype)


def _rope_tables(n_lat):
    t = jnp.arange(n_lat, dtype=jnp.int32)
    row = (t // GRID_W).astype(F32)
    col = (t % GRID_W).astype(F32)
    inv_freq = ROPE_THETA ** (-jnp.arange(0, AXIS_DIM, 2, dtype=F32) / AXIS_DIM)
    ang = jnp.concatenate([jnp.tile(row[:, None] * inv_freq, (1, 2)),
                           jnp.tile(col[:, None] * inv_freq, (1, 2))], axis=-1)
    lane = jnp.arange(HEAD_DIM, dtype=jnp.int32)
    first = (lane % AXIS_DIM) < (AXIS_DIM // 2)
    cos = jnp.cos(ang)
    sin = jnp.sin(ang)
    sin_a = jnp.where(first, -sin, 0.0)
    sin_b = jnp.where(first, 0.0, sin)
    return cos, sin_a, sin_b


def _head_prep(x, g, cos, sin_a, sin_b, rope):
    y = _rms(x) * g
    if rope:
        up = pltpu.roll(y, HEAD_DIM - AXIS_DIM // 2, axis=1)
        dn = pltpu.roll(y, AXIS_DIM // 2, axis=1)
        y = y * cos + up * sin_a + dn * sin_b
    return y


def _qkv_prep_kernel(q_ref, kv_ref, qg_ref, kg_ref, cos_ref, sa_ref, sb_ref, qo_ref, ko_ref, vo_ref):
    cos, sa, sb = cos_ref[...], sa_ref[...], sb_ref[...]
    for h in range(N_Q_HEADS):
        sl = slice(h * HEAD_DIM, (h + 1) * HEAD_DIM)
        q = _head_prep(q_ref[:, sl], qg_ref[...], cos, sa, sb, True)
        qo_ref[:, sl] = (q * SCORE_SCALE_LOG2).astype(qo_ref.dtype)
    for h in range(N_KV_HEADS):
        sl = slice(h * HEAD_DIM, (h + 1) * HEAD_DIM)
        ko_ref[:, sl] = _head_prep(kv_ref[:, sl], kg_ref[...], cos, sa, sb, True).astype(ko_ref.dtype)
    vo_ref[...] = kv_ref[:, KV_WIDTH:].astype(vo_ref.dtype)


def _qkv_prep(p, q_g, k_g, tables):
    tr = ROW_TILE
    rows = p.shape[0]
    cos, sin_a, sin_b = tables
    nt = cos.shape[0] // tr
    tab = pl.BlockSpec((tr, HEAD_DIM), lambda i: (i % nt, 0))
    vec = pl.BlockSpec((1, HEAD_DIM), lambda i: (0, 0))
    kv_out = pl.BlockSpec((tr, KV_WIDTH), lambda i: (i, 0))
    return pl.pallas_call(
        _qkv_prep_kernel,
        grid=(rows // tr,),
        in_specs=[pl.BlockSpec((tr, ATTN_WIDTH), lambda i: (i, 0)),
                  pl.BlockSpec((tr, 2 * KV_WIDTH), lambda i: (i, Q_END // (2 * KV_WIDTH))),
                  vec, vec, tab, tab, tab],
        out_specs=[pl.BlockSpec((tr, ATTN_WIDTH), lambda i: (i, 0)), kv_out, kv_out],
        out_shape=[jax.ShapeDtypeStruct((rows, ATTN_WIDTH), BF16),
                   jax.ShapeDtypeStruct((rows, KV_WIDTH), BF16),
                   jax.ShapeDtypeStruct((rows, KV_WIDTH), BF16)],
        compiler_params=_cparams(1),
        name="qkv_prep",
    )(p, p, q_g, k_g, cos, sin_a, sin_b)


def _kv_prep_ctx_kernel(kv_ref, kg_ref, ko_ref, vo_ref):
    for h in range(N_KV_HEADS):
        sl = slice(h * HEAD_DIM, (h + 1) * HEAD_DIM)
        ko_ref[:, sl] = _head_prep(kv_ref[:, sl], kg_ref[...], None, None, None, False).astype(ko_ref.dtype)
    vo_ref[...] = kv_ref[:, KV_WIDTH:].astype(vo_ref.dtype)


def _kv_prep_ctx(pc_kv, k_g):
    tr = ROW_TILE
    rows = pc_kv.shape[0]
    kv_out = pl.BlockSpec((tr, KV_WIDTH), lambda i: (i, 0))
    return pl.pallas_call(
        _kv_prep_ctx_kernel,
        grid=(rows // tr,),
        in_specs=[pl.BlockSpec((tr, 2 * KV_WIDTH), lambda i: (i, 0)),
                  pl.BlockSpec((1, HEAD_DIM), lambda i: (0, 0))],
        out_specs=[kv_out, kv_out],
        out_shape=[jax.ShapeDtypeStruct((rows, KV_WIDTH), BF16)] * 2,
        compiler_params=_cparams(1),
        name="kv_prep_ctx",
    )(pc_kv, k_g)


def _attn_kernel(q_ref, k_ref, v_ref, o_ref):
    k = k_ref[...]
    v = v_ref[...]
    for g in range(Q_PER_KV):
        sl = slice(g * HEAD_DIM, (g + 1) * HEAD_DIM)
        s = lax.dot_general(q_ref[:, sl], k, (((1,), (1,)), ((), ())), preferred_element_type=F32)
        m = jnp.max(s, axis=-1, keepdims=True)
        p = jnp.exp2(s - m)
        l = jnp.sum(p, axis=-1, keepdims=True)
        o = jnp.dot(p.astype(BF16), v, preferred_element_type=F32)
        o_ref[:, sl] = (o / l).astype(o_ref.dtype)


def _attention(q, k_all, v_all, n_lat):
    tq = 512
    b, t, _ = k_all.shape
    nq = n_lat // tq
    gw = Q_PER_KV * HEAD_DIM
    return pl.pallas_call(
        _attn_kernel,
        grid=(b, N_KV_HEADS, nq),
        in_specs=[pl.BlockSpec((tq, gw), lambda bi, h, i: (bi * nq + i, h)),
                  pl.BlockSpec((None, t, HEAD_DIM), lambda bi, h, i: (bi, 0, h)),
                  pl.BlockSpec((None, t, HEAD_DIM), lambda bi, h, i: (bi, 0, h))],
        out_specs=pl.BlockSpec((tq, gw), lambda bi, h, i: (bi * nq + i, h)),
        out_shape=jax.ShapeDtypeStruct(q.shape, BF16),
        compiler_params=_cparams(3),
        name="attention",
    )(q, k_all, v_all)


def _filter_kernel(z_ref, w1_ref, b1_ref, w2_ref, b2_ref, w3_ref, b3_ref, fr_ref, wo_ref, t_ref, ad_ref, o_ref,
                   a_ref):
    j = pl.program_id(0)
    hi = lax.Precision.HIGHEST

    @pl.when(j == 0)
    def _():
        fr = fr_ref[...]
        a = jnp.sin(fr * (jnp.dot(z_ref[...], w1_ref[...], precision=hi, preferred_element_type=F32)
                          + b1_ref[...]))
        a = jnp.sin(fr * (jnp.dot(a, w2_ref[...], precision=hi, preferred_element_type=F32) + b2_ref[...]))
        a_ref[...] = jnp.sin(fr * (jnp.dot(a, w3_ref[...], precision=hi, preferred_element_type=F32)
                                   + b3_ref[...]))

    h = jnp.dot(a_ref[...], wo_ref[...], precision=hi, preferred_element_type=F32)
    h = h * jnp.exp(-t_ref[...] * ad_ref[...])
    backward = (j // 2) % 2 == 1
    row = lax.broadcasted_iota(jnp.int32, h.shape, 0)
    o_ref[...] = jnp.where(jnp.logical_and(backward, row == 0), 0.0, h)


def _pad2(a, rows, cols):
    return jnp.pad(a, ((0, rows - a.shape[0]), (0, cols - a.shape[1])))


def _hyena_filters(n, w1, b1, w2, b2, w3, b3, freq, w_out):
    t = jnp.linspace(0.0, 1.0, n, dtype=F32)[:, None]
    bands = (FILTER_EMB - 1) // 2
    w = 2.0 * math.pi * jnp.arange(n, dtype=F32)[:, None] / n
    f = jnp.linspace(1e-4, bands - 1, bands, dtype=F32)[None, :]
    z = jnp.concatenate([t, jnp.cos(f * w), -jnp.sin(f * w)], axis=-1)
    max_decay = math.log(DECAY_TARGET) / FAST_DECAY_PCT
    min_decay = math.log(DECAY_TARGET) / SLOW_DECAY_PCT
    ad = jnp.abs(jnp.linspace(min_decay, max_decay, HYENA_WIDTH, dtype=F32))[None, :]
    p = FILT_PAD
    tn = 1024
    n_out = w_out.shape[1]
    full = lambda shape: pl.BlockSpec(shape, lambda j: (0, 0))
    return pl.pallas_call(
        _filter_kernel,
        grid=(n_out // tn,),
        in_specs=[full((n, p)), full((p, p)), full((1, p)), full((p, p)), full((1, p)), full((p, p)),
                  full((1, p)), full((1, p)),
                  pl.BlockSpec((p, tn), lambda j: (0, j)),
                  full((n, 1)),
                  pl.BlockSpec((1, tn), lambda j: (0, j % (HYENA_WIDTH // tn)))],
        out_specs=pl.BlockSpec((n, tn), lambda j: (0, j)),
        out_shape=jax.ShapeDtypeStruct((n, n_out), F32),
        scratch_shapes=[pltpu.VMEM((n, p), F32)],
        compiler_params=_cparams(1),
        name="hyena_filters",
    )(_pad2(z, n, p), _pad2(w1, p, p), _pad2(b1[None, :], 1, p), _pad2(w2, p, p), _pad2(b2[None, :], 1, p),
      _pad2(w3, p, p), _pad2(b3[None, :], 1, p), _pad2(freq[None, :], 1, p), _pad2(w_out, p, n_out), t, ad)


def _dft_matrix_kernel(ca_ref, sa_ref, cb_ref, sb_ref, alt_ref, cs_ref, fre_ref, fim_ref, gre_ref, gim_ref):
    i = pl.program_id(0)
    ca, sa = ca_ref[...], sa_ref[...]
    cb, sb = cb_ref[...], sb_ref[...]
    cos = ca * cb - sa * sb
    sin = sa * cb + ca * sb
    cs = cs_ref[...]
    row = lax.broadcasted_iota(jnp.int32, cos.shape, 0)
    col = lax.broadcasted_iota(jnp.int32, cos.shape, 1)
    nyquist_row = jnp.logical_and(i == 0, row == 0)
    alt_rows = jnp.where(row % 2 == 0, 1.0, -1.0)
    fre_ref[...] = cos.astype(fre_ref.dtype)
    fim_ref[...] = jnp.where(nyquist_row, alt_ref[...], -sin).astype(fim_ref.dtype)
    gre_ref[...] = (cos * cs).astype(gre_ref.dtype)
    gim_ref[...] = (jnp.where(col == 0, alt_rows, -sin) * cs).astype(gim_ref.dtype)


def _dft_matrices(n):
    big = 2 * n
    rb = DFT_ROWS
    t = jnp.arange(n, dtype=jnp.int32)[None, :]
    unit = 2.0 * math.pi / big
    coarse = ((rb * jnp.arange(n // rb, dtype=jnp.int32)[:, None] * t) % big).astype(F32) * unit
    fine = ((jnp.arange(rb, dtype=jnp.int32)[:, None] * t) % big).astype(F32) * unit
    alt = jnp.where(t % 2 == 0, 1.0, -1.0).astype(F32)
    cs = jnp.where(t == 0, 1.0 / big, 2.0 / big).astype(F32)
    coarse_spec = pl.BlockSpec((None, 1, n), lambda i: (i, 0, 0))
    whole = lambda r: pl.BlockSpec((r, n), lambda i: (0, 0))
    out_spec = pl.BlockSpec((rb, n), lambda i: (i, 0))
    return pl.pallas_call(
        _dft_matrix_kernel,
        grid=(n // rb,),
        in_specs=[coarse_spec, coarse_spec, whole(rb), whole(rb), whole(1), whole(1)],
        out_specs=[out_spec] * 4,
        out_shape=[jax.ShapeDtypeStruct((n, n), BF16)] * 4,
        compiler_params=_cparams(1),
        name="dft_matrices",
    )(jnp.cos(coarse)[:, None, :], jnp.sin(coarse)[:, None, :], jnp.cos(fine), jnp.sin(fine), alt, cs)


def _spectrum_kernel(fre_ref, fim_ref, hf_ref, hb_ref, kre_ref, kim_ref):
    i = pl.program_id(0)
    hf = hf_ref[...]
    hb = hb_ref[...]
    h_sum = (hf + hb).astype(BF16)
    h_dif = (hf - hb).astype(BF16)
    kre_ref[...] = jnp.dot(fre_ref[...], h_sum, preferred_element_type=F32)
    kim = jnp.dot(fim_ref[...], h_dif, preferred_element_type=F32)
    nyq = jnp.dot(fim_ref[pl.ds(0, BF16_SUBLANES), :], h_sum, preferred_element_type=F32)[0:1, :]
    row = lax.broadcasted_iota(jnp.int32, kim.shape, 0)
    kim_ref[...] = jnp.where(jnp.logical_and(i == 0, row == 0), nyq, kim)


def _filter_spectra(h_time, f_re, f_im):
    n = h_time.shape[0]
    tf, tc = 1024, 512
    per = HYENA_WIDTH // tc
    return pl.pallas_call(
        _spectrum_kernel,
        grid=(n // tf, 2 * per),
        in_specs=[pl.BlockSpec((tf, n), lambda i, j: (i, 0)),
                  pl.BlockSpec((tf, n), lambda i, j: (i, 0)),
                  pl.BlockSpec((n, tc), lambda i, j: (0, (j // per) * 2 * per + j % per)),
                  pl.BlockSpec((n, tc), lambda i, j: (0, (j // per) * 2 * per + per + j % per))],
        out_specs=[pl.BlockSpec((tf, tc), lambda i, j: (i, j))] * 2,
        out_shape=[jax.ShapeDtypeStruct((n, 2 * HYENA_WIDTH), F32)] * 2,
        compiler_params=_cparams(2),
        name="filter_spectra",
    )(f_re, f_im, h_time, h_time)


def _short_conv_kernel(p_ref, w_ref, b_ref, u_ref, v16_ref):
    j = pl.program_id(1)
    x = p_ref[...]
    n = x.shape[0]
    row = lax.broadcasted_iota(jnp.int32, x.shape, 0)
    prev = jnp.where(row == 0, 0.0, pltpu.roll(x, 1, axis=0))
    nxt = jnp.where(row == n - 1, 0.0, pltpu.roll(x, n - 1, axis=0))
    u = prev * w_ref[0:1, :] + x * w_ref[1:2, :] + nxt * w_ref[2:3, :] + b_ref[...]
    u_ref[...] = u

    @pl.when(j < HYENA_WIDTH // u.shape[1])
    def _():
        v16_ref[...] = u.astype(v16_ref.dtype)


def _short_conv(p3, w, b):
    bsz, n, _ = p3.shape
    tc = 512
    nj = 3 * HYENA_WIDTH // tc
    nv = HYENA_WIDTH // tc
    off = V_END // tc
    return pl.pallas_call(
        _short_conv_kernel,
        grid=(bsz, nj),
        in_specs=[pl.BlockSpec((None, n, tc), lambda bi, j: (bi, 0, off + j)),
                  pl.BlockSpec((3, tc), lambda bi, j: (0, j)),
                  pl.BlockSpec((1, tc), lambda bi, j: (0, j))],
        out_specs=[pl.BlockSpec((None, n, tc), lambda bi, j: (bi, 0, j)),
                   pl.BlockSpec((None, n, tc), lambda bi, j: (bi, 0, jnp.minimum(j, nv - 1)))],
        out_shape=[jax.ShapeDtypeStruct((bsz, n, 3 * HYENA_WIDTH), F32),
                   jax.ShapeDtypeStruct((bsz, n, HYENA_WIDTH), BF16)],
        compiler_params=_cparams(2),
        name="short_conv",
    )(p3, w, b)


def _dft_fwd_kernel(fre_ref, fim_ref, u_ref, kre_ref, kim_ref, yre_ref, yim_ref):
    f = pl.program_id(2)
    u = u_ref[...]
    ure = jnp.dot(fre_ref[...], u, preferred_element_type=F32)
    uim = jnp.dot(fim_ref[...], u, preferred_element_type=F32)
    kre = kre_ref[...]
    kim = kim_ref[...]
    row = lax.broadcasted_iota(jnp.int32, ure.shape, 0)
    real_pair = jnp.logical_and(f == 0, row == 0)
    yre_ref[...] = jnp.where(real_pair, ure * kre, ure * kre - uim * kim).astype(yre_ref.dtype)
    yim_ref[...] = jnp.where(real_pair, uim * kim, ure * kim + uim * kre).astype(yim_ref.dtype)


def _dft_forward(u16, f_re, f_im, k_re, k_im, order):
    bsz, n, ch = u16.shape
    tf, tc = 1024, 512
    nc = ch // tc
    return pl.pallas_call(
        _dft_fwd_kernel,
        grid=(bsz, nc, n // tf),
        in_specs=[pl.BlockSpec((tf, n), lambda bi, c, f: (f, 0)),
                  pl.BlockSpec((tf, n), lambda bi, c, f: (f, 0)),
                  pl.BlockSpec((None, n, tc), lambda bi, c, f: (bi, 0, c)),
                  pl.BlockSpec((tf, tc), lambda bi, c, f: (f, order * nc + c)),
                  pl.BlockSpec((tf, tc), lambda bi, c, f: (f, order * nc + c))],
        out_specs=[pl.BlockSpec((None, tf, tc), lambda bi, c, f: (bi, f, c))] * 2,
        out_shape=[jax.ShapeDtypeStruct((bsz, n, ch), BF16)] * 2,
        compiler_params=_cparams(3),
        name="dft_forward",
    )(f_re, f_im, u16, k_re, k_im)


def _dft_inv_kernel(gre_ref, gim_ref, yre_ref, yim_ref, u_ref, gate_ref, bias_ref, *o_refs):
    y = (jnp.dot(gre_ref[...], yre_ref[...], preferred_element_type=F32)
         + jnp.dot(gim_ref[...], yim_ref[...], preferred_element_type=F32))
    z = gate_ref[...] * (y + u_ref[...] * bias_ref[...])
    for o in o_refs:
        o[...] = z.astype(o.dtype)


def _dft_inverse(y_re, y_im, g_re, g_im, u_arr, u_blk0, gate_arr, gate_blk0, bias, out_dtypes):
    bsz, n, ch = y_re.shape
    tt, tc = 1024, 512
    nc = ch // tc
    return pl.pallas_call(
        _dft_inv_kernel,
        grid=(bsz, nc, n // tt),
        in_specs=[pl.BlockSpec((tt, n), lambda bi, c, t: (t, 0)),
                  pl.BlockSpec((tt, n), lambda bi, c, t: (t, 0)),
                  pl.BlockSpec((None, n, tc), lambda bi, c, t: (bi, 0, c)),
                  pl.BlockSpec((None, n, tc), lambda bi, c, t: (bi, 0, c)),
                  pl.BlockSpec((None, tt, tc), lambda bi, c, t: (bi, t, u_blk0 * nc + c)),
                  pl.BlockSpec((None, tt, tc), lambda bi, c, t: (bi, t, gate_blk0 * nc + c)),
                  pl.BlockSpec((1, tc), lambda bi, c, t: (0, c))],
        out_specs=[pl.BlockSpec((None, tt, tc), lambda bi, c, t: (bi, t, c))] * len(out_dtypes),
        out_shape=[jax.ShapeDtypeStruct((bsz, n, ch), dt) for dt in out_dtypes],
        compiler_params=_cparams(3),
        name="dft_inverse",
    )(g_re, g_im, y_re, y_im, u_arr, gate_arr, bias)


def _ffn(h, w_gate, w_up, w_down, lead, rows, side=None, side_args=()):
    n_tiles = pl.cdiv(D_FF, MM_TN)
    outs = _matmul_wres([h], [w_gate, w_up], [(0, 0), (0, 1)], _ep_swiglu, D_FF, [BF16], tm=MM_TM, tn=MM_TN,
                        m_tiles=rows // MM_TM, n_tiles=n_tiles, k=D_MODEL, ch=MM_CHUNK, lead=[lead, lead],
                        last_width=D_FF - (n_tiles - 1) * MM_TN, side=side, side_args=side_args,
                        name="ffn_gate_up")
    f = _matmul_wres([outs[0]], [w_down], [(0, 0)], _ep_store, D_MODEL, [F32], tm=DOWN_TM, tn=MM_TN,
                     m_tiles=rows // DOWN_TM, n_tiles=D_MODEL // MM_TN, k=D_FF, ch=DOWN_CHUNK, lead=[lead],
                     name="ffn_down")[0]
    return f, (outs[1] if side is not None else None)


def kernel(x, c, ctx, c_ctx, w_ada, b_ada, pre_g, post_g, ffn_w_gate, ffn_w_up, ffn_w_down, w_in, q_norm_g,
           k_norm_g, short_w, short_b, filt_w1, filt_b1, filt_w2, filt_b2, filt_w3, filt_b3, filt_freq,
           filt_w_out, hyena_bias, w_br_attn, w_br_hyena, w_out):
    bsz, n_lat, d = x.shape
    n_ctx = ctx.shape[1]
    rows_x, rows_c = bsz * n_lat, bsz * n_ctx
    x2d = x.reshape(rows_x, d)
    c2d = ctx.reshape(rows_c, d)
    l = 0

    cond = jnp.concatenate([c, c_ctx[None, :], jnp.zeros((3, d), F32)], axis=0)
    n_first = 2 * d
    bias = b_ada[l][None, :]
    mods_first = _ada(cond, w_ada, bias, l, n_first)
    pre = [pre_g[l, s][None, :] for s in range(3)]
    post = [post_g[l, s][None, :] for s in range(3)]

    h0 = _modulate_first(x2d, c2d, mods_first[:5].reshape(5, 2, d), pre[0], shift_i=0, scale_i=1)
    side = _SidePlan(lead=(l,), k=d, ts=ADA_SIDE_TILE, n_tiles=(N_MOD * d - n_first) // ADA_SIDE_TILE,
                     col0=n_first)
    f1, mods_rest = _ffn(h0, ffn_w_gate, ffn_w_up, ffn_w_down, (l, 0), rows_x + rows_c, side=side,
                         side_args=(cond, w_ada, bias[:, n_first:]))
    mods = jnp.concatenate([mods_first, mods_rest], axis=1)[:5].reshape(5, N_MOD, d)
    x1, h1 = _residual(f1, x2d, c2d, mods, post[0], pre[1], gate_i=2, weight=MACARON_W, shift_i=3, scale_i=4)

    tm, tn = MM_TM, MM_TN
    p = _matmul_wres([h1], [w_in], [(0, 0)], _ep_store, IN_COLS, [F32], tm=tm, tn=IN_TN, m_tiles=rows_x // tm,
                     n_tiles=IN_COLS // IN_TN, k=d, ch=IN_CHUNK, lead=[(l,)], name="in_proj")[0]
    pc_kv = _matmul_wres([h1], [w_in], [(0, 0)], _ep_store, 2 * KV_WIDTH, [F32], tm=CTX_TM, tn=tn,
                         m_tiles=rows_c // CTX_TM, n_tiles=2 * KV_WIDTH // tn, k=d, ch=MM_CHUNK, lead=[(l,)],
                         col0=Q_END, a_blk0=rows_x // CTX_TM, name="in_proj_ctx")[0]

    tables = _rope_tables(n_lat)
    q16, k16, v16 = _qkv_prep(p, q_norm_g[l][None, :], k_norm_g[l][None, :], tables)
    kc16, vc16 = _kv_prep_ctx(pc_kv, k_norm_g[l][None, :])
    k_all = jnp.concatenate([kc16.reshape(bsz, n_ctx, KV_WIDTH), k16.reshape(bsz, n_lat, KV_WIDTH)], axis=1)
    v_all = jnp.concatenate([vc16.reshape(bsz, n_ctx, KV_WIDTH), v16.reshape(bsz, n_lat, KV_WIDTH)], axis=1)
    attn_o = _attention(q16, k_all, v_all, n_lat)

    f_re, f_im, g_re, g_im = _dft_matrices(n_lat)
    h_time = _hyena_filters(n_lat, filt_w1[l], filt_b1[l], filt_w2[l], filt_b2[l], filt_w3[l], filt_b3[l],
                            filt_freq[l], filt_w_out[l])
    k_re, k_im = _filter_spectra(h_time, f_re, f_im)
    p3 = p.reshape(bsz, n_lat, IN_COLS)
    u, v16h = _short_conv(p3, short_w[l], short_b[l][None, :])
    y_re, y_im = _dft_forward(v16h, f_re, f_im, k_re, k_im, order=0)
    z, z16 = _dft_inverse(y_re, y_im, g_re, g_im, u, 0, u, 1, hyena_bias[l, 0][None, :], [F32, BF16])
    y_re, y_im = _dft_forward(z16, f_re, f_im, k_re, k_im, order=1)
    hy_o = _dft_inverse(y_re, y_im, g_re, g_im, z, 0, u, 2, hyena_bias[l, 1][None, :], [BF16])[0]

    gate_spec = lambda off: pl.BlockSpec((tm, tn), lambda j, i: (i, off + j))
    merged = _matmul_wres([attn_o, hy_o.reshape(rows_x, HYENA_WIDTH)], [w_br_attn, w_br_hyena], [(0, 0), (1, 1)],
                          _ep_merge, d, [BF16], tm=tm, tn=tn, m_tiles=rows_x // tm, n_tiles=d // tn,
                          k=ATTN_WIDTH, ch=MM_CHUNK, lead=[(l,), (l,)], extras=(p, p),
                          extra_specs=(gate_spec(HY_END // tn), gate_spec((HY_END + d) // tn)), name="merge")[0]
    out = _matmul_wres([merged], [w_out], [(0, 0)], _ep_store, d, [F32], tm=tm, tn=tn, m_tiles=rows_x // tm,
                       n_tiles=d // tn, k=d, ch=MM_CHUNK, lead=[(l,)], name="out_proj")[0]
    x2, h2 = _residual(out, x1, None, mods, post[1], pre[2], gate_i=5, weight=1.0, shift_i=6, scale_i=7)

    f2, _ = _ffn(h2, ffn_w_gate, ffn_w_up, ffn_w_down, (l, 1), rows_x)
    x3 = _residual(f2, x2, None, mods, post[2], None, gate_i=8, weight=MACARON_W)[0]
    return x3.reshape(bsz, n_lat, d)
```

```python
import functools
import math
from typing import NamedTuple

import jax
import jax.numpy as jnp
from jax import lax
from jax.experimental import pallas as pl
from jax.experimental.pallas import tpu as pltpu

F32 = jnp.float32
BF16 = jnp.bfloat16
BRANCH_DTYPE = BF16

D_MODEL = 4096
GRID_W = 64
HEAD_DIM = 128
N_Q_HEADS = 16
N_KV_HEADS = 4
Q_PER_KV = N_Q_HEADS // N_KV_HEADS
ATTN_WIDTH = N_Q_HEADS * HEAD_DIM
KV_WIDTH = N_KV_HEADS * HEAD_DIM
ROPE_THETA = 10000.0
AXIS_DIM = HEAD_DIM // 2
HYENA_WIDTH = D_MODEL // 2
FILTER_EMB = 33
FILTER_HIDDEN = 64
DECAY_TARGET = 1e-2
FAST_DECAY_PCT = 0.3
SLOW_DECAY_PCT = 1.5
D_FF = 11008
MACARON_W = 0.5
N_MOD = 9
NORM_EPS = 1e-6
Q_END = ATTN_WIDTH
K_END = Q_END + KV_WIDTH
V_END = K_END + KV_WIDTH
HY_END = V_END + 3 * HYENA_WIDTH
IN_COLS = HY_END + 2 * D_MODEL
SCORE_SCALE_LOG2 = math.log2(math.e) / math.sqrt(HEAD_DIM)

V7X_LANES = 128
BF16_SUBLANES = 16
V7X_VMEM_LIMIT_BYTES = 56 * 1024 * 1024

MM_TM = 1024
MM_TN = 512
MM_CHUNK = 1024
IN_TN = 1024
IN_CHUNK = 512
CTX_TM = 256
DOWN_TM = 512
DOWN_CHUNK = D_FF // 16
ROW_TILE = 256
ADA_SIDE_TILE = 256
DFT_ROWS = 64
FILT_PAD = V7X_LANES


def _cparams(n_axes):
    return pltpu.CompilerParams(dimension_semantics=("arbitrary",) * n_axes,
                                vmem_limit_bytes=V7X_VMEM_LIMIT_BYTES)


def _rms(x):
    return x * lax.rsqrt(jnp.mean(x * x, axis=-1, keepdims=True) + NORM_EPS)


def _ada_kernel(c_ref, w_ref, b_ref, o_ref):
    c = c_ref[...]
    s = (c * jax.nn.sigmoid(c)).astype(BF16)
    o_ref[...] = jnp.dot(s, w_ref[...].astype(BF16), preferred_element_type=F32) + b_ref[...]


def _ada(cond, w, b, layer, n):
    tn = 512
    return pl.pallas_call(
        _ada_kernel,
        grid=(n // tn,),
        in_specs=[pl.BlockSpec((8, D_MODEL), lambda j: (0, 0)),
                  pl.BlockSpec((None, D_MODEL, tn), lambda j: (layer, 0, j)),
                  pl.BlockSpec((1, tn), lambda j: (0, j))],
        out_specs=pl.BlockSpec((8, tn), lambda j: (0, j)),
        out_shape=jax.ShapeDtypeStruct((8, n), F32),
        compiler_params=_cparams(1),
        name="ada",
    )(cond, w, b)


def _mod_index(i, rows_per_batch_tiles):
    return jnp.minimum(i // rows_per_batch_tiles, 4)


def _modulate_kernel(x_ref, c_ref, m_ref, g_ref, o_ref, *, nx, shift_i, scale_i):
    i = pl.program_id(0)

    def body(src):
        y = _rms(src[...]) * g_ref[...]
        o_ref[...] = (y * (1.0 + m_ref[0, scale_i:scale_i + 1, :])
                      + m_ref[0, shift_i:shift_i + 1, :]).astype(o_ref.dtype)

    pl.when(i < nx)(lambda: body(x_ref))
    pl.when(i >= nx)(lambda: body(c_ref))


def _modulate_first(x2d, c2d, mods, g, shift_i, scale_i):
    tr = ROW_TILE
    nx, nc = x2d.shape[0] // tr, c2d.shape[0] // tr
    per_batch = (x2d.shape[0] // 4) // tr
    return pl.pallas_call(
        functools.partial(_modulate_kernel, nx=nx, shift_i=shift_i, scale_i=scale_i),
        grid=(nx + nc,),
        in_specs=[pl.BlockSpec((tr, D_MODEL), lambda i: (jnp.minimum(i, nx - 1), 0)),
                  pl.BlockSpec((tr, D_MODEL), lambda i: (jnp.maximum(i - nx, 0), 0)),
                  pl.BlockSpec((1, mods.shape[1], D_MODEL), lambda i: (_mod_index(i, per_batch), 0, 0)),
                  pl.BlockSpec((1, D_MODEL), lambda i: (0, 0))],
        out_specs=pl.BlockSpec((tr, D_MODEL), lambda i: (i, 0)),
        out_shape=jax.ShapeDtypeStruct(((nx + nc) * tr, D_MODEL), BF16),
        compiler_params=_cparams(1),
        name="modulate_first",
    )(x2d, c2d, mods, g)


def _resid_kernel(*refs, nx, two_source, gate_i, weight, shift_i, scale_i, with_next):
    it = iter(refs)
    f_ref = next(it)
    x_ref = next(it)
    c_ref = next(it) if two_source else None
    m_ref = next(it)
    pg_ref = next(it)
    ng_ref = next(it) if with_next else None
    xo_ref = next(it)
    ho_ref = next(it) if with_next else None
    i = pl.program_id(0)

    def body(src):
        y = _rms(f_ref[...].astype(F32)) * pg_ref[...]
        xn = src[...] + (weight * m_ref[0, gate_i:gate_i + 1, :]) * y
        xo_ref[...] = xn
        if with_next:
            h = _rms(xn) * ng_ref[...]
            ho_ref[...] = (h * (1.0 + m_ref[0, scale_i:scale_i + 1, :])
                           + m_ref[0, shift_i:shift_i + 1, :]).astype(ho_ref.dtype)

    if two_source:
        pl.when(i < nx)(lambda: body(x_ref))
        pl.when(i >= nx)(lambda: body(c_ref))
    else:
        body(x_ref)


def _residual(f, x2d, c2d, mods, post_g, next_g, *, gate_i, weight, shift_i=0, scale_i=0):
    tr = ROW_TILE
    rows = f.shape[0]
    n = rows // tr
    two_source = c2d is not None
    with_next = next_g is not None
    nx = x2d.shape[0] // tr if two_source else n
    per_batch = (8192 // 4) // tr
    row = pl.BlockSpec((tr, D_MODEL), lambda i: (i, 0))
    vec = pl.BlockSpec((1, D_MODEL), lambda i: (0, 0))
    in_specs = [row]
    args = [f]
    if two_source:
        in_specs += [pl.BlockSpec((tr, D_MODEL), lambda i: (jnp.minimum(i, nx - 1), 0)),
                     pl.BlockSpec((tr, D_MODEL), lambda i: (jnp.maximum(i - nx, 0), 0))]
        args += [x2d, c2d]
    else:
        in_specs += [row]
        args += [x2d]
    in_specs += [pl.BlockSpec((1, N_MOD, D_MODEL), lambda i: (_mod_index(i, per_batch), 0, 0)), vec]
    args += [mods, post_g]
    out_specs = [row]
    out_shape = [jax.ShapeDtypeStruct((rows, D_MODEL), F32)]
    if with_next:
        in_specs += [vec]
        args += [next_g]
        out_specs += [row]
        out_shape += [jax.ShapeDtypeStruct((rows, D_MODEL), BF16)]
    return pl.pallas_call(
        functools.partial(_resid_kernel, nx=nx, two_source=two_source, gate_i=gate_i, weight=weight,
                          shift_i=shift_i, scale_i=scale_i, with_next=with_next),
        grid=(n,),
        in_specs=in_specs,
        out_specs=out_specs,
        out_shape=out_shape,
        compiler_params=_cparams(1),
        name="residual",
    )(*args)


class _WeightPlan(NamedTuple):
    k: int
    ch: int
    tn: int
    n_tiles: int
    m_tiles: int
    col0: int
    last_width: int
    lead: tuple


def _wres_kernel(*refs, n_a, n_w, pairs, n_extra, n_out, epilogue, plan, side):
    it = iter(refs)
    a_refs = [next(it) for _ in range(n_a)]
    w_refs = [next(it) for _ in range(n_w)]
    e_refs = [next(it) for _ in range(n_extra)]
    if side is not None:
        side_c_ref, side_w_ref, side_b_ref = next(it), next(it), next(it)
    o_refs = [next(it) for _ in range(n_out)]
    if side is not None:
        side_o_ref = next(it)
    wbf, stage, sem = next(it), next(it), next(it)
    if side is not None:
        side_stage, side_sem = next(it), next(it)
    j = pl.program_id(0)
    i = pl.program_id(1)
    slot = j % 2
    per_w = plan.k // plan.ch
    n_chunks = n_w * per_w
    ragged = plan.last_width != plan.tn

    def chunk_copy(tile, c, width):
        w, r = divmod(c, per_w)
        col = plan.col0 + tile * plan.tn
        if not isinstance(col, int):
            col = pl.multiple_of(col, V7X_LANES)
        src = w_refs[w].at[(*plan.lead[w], pl.ds(r * plan.ch, plan.ch), pl.ds(col, width))]
        return pltpu.make_async_copy(src, stage.at[c % 2, :, pl.ds(0, width)], sem.at[c % 2])

    def width_variants(tile):
        if not ragged:
            return [(plan.tn, tile < plan.n_tiles)]
        return [(plan.tn, tile < plan.n_tiles - 1), (plan.last_width, tile == plan.n_tiles - 1)]

    @pl.when(jnp.logical_and(j == 0, i == 0))
    def _():
        chunk_copy(0, 0, plan.tn).start()
        for c in range(n_chunks):
            w, r = divmod(c, per_w)
            chunk_copy(0, c, plan.tn).wait()
            if c + 1 < n_chunks:
                chunk_copy(0, c + 1, plan.tn).start()
            wbf[0, w, pl.ds(r * plan.ch, plan.ch), :] = stage[c % 2].astype(BF16)
        second = plan.last_width if plan.n_tiles == 2 else plan.tn
        chunk_copy(1, 0, second).start()

    for width, cond in width_variants(j + 1):
        for c in range(n_chunks):
            @pl.when(jnp.logical_and(cond, i == c))
            def _(width=width, c=c):
                chunk_copy(j + 1, c, width).wait()
                if c + 1 < n_chunks:
                    chunk_copy(j + 1, c + 1, width).start()

    for width, cond in width_variants(j + 2):
        @pl.when(jnp.logical_and(cond, i == plan.m_tiles - 1))
        def _(width=width):
            chunk_copy(j + 2, 0, width).start()

    if side is not None:
        _side_stream_dma(side, side_w_ref, side_stage, side_sem, j * plan.m_tiles + i)

    def compute(width):
        c_now = jnp.minimum(i, n_chunks - 1)
        w_now = c_now // per_w
        row_now = pl.multiple_of((c_now % per_w) * plan.ch, BF16_SUBLANES)
        wbf[1 - slot, w_now, pl.ds(row_now, plan.ch), :] = stage[c_now % 2].astype(BF16)
        if side is not None:
            _side_stream_compute(side, side_c_ref, side_b_ref, side_o_ref, side_stage, j * plan.m_tiles + i)
        prods = [jnp.dot(a_refs[ai][...], wbf[slot, wi, :, pl.ds(0, width)], preferred_element_type=F32)
                 for ai, wi in pairs]
        epilogue(prods, e_refs, o_refs, width)

    if ragged:
        pl.when(j < plan.n_tiles - 1)(lambda: compute(plan.tn))
        pl.when(j == plan.n_tiles - 1)(lambda: compute(plan.last_width))
    else:
        compute(plan.tn)


class _SidePlan(NamedTuple):
    lead: tuple
    k: int
    ts: int
    n_tiles: int
    col0: int


def _side_stream_dma(side, w_ref, stage, sem, g):
    def copy(tile, slot):
        col = pl.multiple_of(side.col0 + tile * side.ts, V7X_LANES)
        return pltpu.make_async_copy(w_ref.at[(*side.lead, slice(None), pl.ds(col, side.ts))],
                                     stage.at[slot], sem.at[slot])

    pl.when(g == 0)(lambda: copy(g, 0).start())
    for slot in range(2):
        pl.when(jnp.logical_and(g < side.n_tiles, g % 2 == slot))(lambda slot=slot: copy(g, slot).wait())
    for slot in range(2):
        pl.when(jnp.logical_and(g + 1 < side.n_tiles, (g + 1) % 2 == slot))(
            lambda slot=slot: copy(g + 1, slot).start())


def _side_stream_compute(side, c_ref, b_ref, o_ref, stage, g):
    t = jnp.minimum(g, side.n_tiles - 1)
    c = c_ref[...]
    s = (c * jax.nn.sigmoid(c)).astype(BF16)
    o_ref[...] = jnp.dot(s, stage[t % 2].astype(BF16), preferred_element_type=F32) + b_ref[...]


def _matmul_wres(a_list, w_list, pairs, epilogue, out_cols, out_dtypes, *, tm, tn, m_tiles, n_tiles, k, ch,
                 lead, col0=0, last_width=None, a_blk0=0, extras=(), extra_specs=(), side=None, side_args=(),
                 name="matmul"):
    plan = _WeightPlan(k=k, ch=ch, tn=tn, n_tiles=n_tiles, m_tiles=m_tiles, col0=col0,
                       last_width=last_width or tn, lead=tuple(lead))
    n_chunks = len(w_list) * (k // ch)
    assert n_chunks <= m_tiles and n_chunks % 2 == 0 and n_tiles >= 2 and k % ch == 0
    in_specs = [pl.BlockSpec((tm, k), lambda j, i: (i + a_blk0, 0)) for _ in a_list]
    in_specs += [pl.BlockSpec(memory_space=pl.ANY) for _ in w_list]
    in_specs += list(extra_specs)
    out_specs = [pl.BlockSpec((tm, tn), lambda j, i: (i, j)) for _ in out_dtypes]
    out_shape = [jax.ShapeDtypeStruct((m_tiles * tm, out_cols), dt) for dt in out_dtypes]
    scratch = [pltpu.VMEM((2, len(w_list), k, tn), BF16), pltpu.VMEM((2, ch, tn), F32),
               pltpu.SemaphoreType.DMA((2,))]
    if side is not None:
        assert side.n_tiles <= n_tiles * m_tiles
        side_tile = lambda j, i: (0, jnp.minimum(j * m_tiles + i, side.n_tiles - 1))
        in_specs += [pl.BlockSpec((8, side.k), lambda j, i: (0, 0)), pl.BlockSpec(memory_space=pl.ANY),
                     pl.BlockSpec((1, side.ts), side_tile)]
        out_specs += [pl.BlockSpec((8, side.ts), side_tile)]
        out_shape += [jax.ShapeDtypeStruct((8, side.n_tiles * side.ts), F32)]
        scratch += [pltpu.VMEM((2, side.k, side.ts), F32), pltpu.SemaphoreType.DMA((2,))]
    return pl.pallas_call(
        functools.partial(_wres_kernel, n_a=len(a_list), n_w=len(w_list), pairs=tuple(pairs),
                          n_extra=len(extras), n_out=len(out_dtypes), epilogue=epilogue, plan=plan, side=side),
        grid=(n_tiles, m_tiles),
        in_specs=in_specs,
        out_specs=out_specs,
        out_shape=out_shape,
        scratch_shapes=scratch,
        compiler_params=_cparams(2),
        name=name,
    )(*a_list, *w_list, *extras, *side_args)


def _ep_store(accs, e_refs, o_refs, width):
    o_refs[0][:, pl.ds(0, width)] = accs[0].astype(o_refs[0].dtype)


def _ep_swiglu(accs, e_refs, o_refs, width):
    g, u = accs
    o_refs[0][:, pl.ds(0, width)] = (g * jax.nn.sigmoid(g) * u).astype(o_refs[0].dtype)


def _ep_merge(accs, e_refs, o_refs, width):
    a, h = accs
    ga, gh = e_refs
    o_refs[0][...] = (jax.nn.sigmoid(ga[...]) * a + jax.nn.sigmoid(gh[...]) * h).astype(o_refs[0].dtype)


def _rope_tables(n_lat):
    t = jnp.arange(n_lat, dtype=jnp.int32)
    row = (t // GRID_W).astype(F32)
    col = (t % GRID_W).astype(F32)
    inv_freq = ROPE_THETA ** (-jnp.arange(0, AXIS_DIM, 2, dtype=F32) / AXIS_DIM)
    ang = jnp.concatenate([jnp.tile(row[:, None] * inv_freq, (1, 2)),
                           jnp.tile(col[:, None] * inv_freq, (1, 2))], axis=-1)
    lane = jnp.arange(HEAD_DIM, dtype=jnp.int32)
    first = (lane % AXIS_DIM) < (AXIS_DIM // 2)
    cos = jnp.cos(ang)
    sin = jnp.sin(ang)
    sin_a = jnp.where(first, -sin, 0.0)
    sin_b = jnp.where(first, 0.0, sin)
    return cos, sin_a, sin_b


def _head_prep(x, g, cos, sin_a, sin_b, rope):
    y = _rms(x) * g
    if rope:
        up = pltpu.roll(y, HEAD_DIM - AXIS_DIM // 2, axis=1)
        dn = pltpu.roll(y, AXIS_DIM // 2, axis=1)
        y = y * cos + up * sin_a + dn * sin_b
    return y


def _qkv_prep_kernel(q_ref, kv_ref, qg_ref, kg_ref, cos_ref, sa_ref, sb_ref, qo_ref, ko_ref, vo_ref):
    cos, sa, sb = cos_ref[...], sa_ref[...], sb_ref[...]
    for h in range(N_Q_HEADS):
        sl = slice(h * HEAD_DIM, (h + 1) * HEAD_DIM)
        q = _head_prep(q_ref[:, sl], qg_ref[...], cos, sa, sb, True)
        qo_ref[:, sl] = (q * SCORE_SCALE_LOG2).astype(qo_ref.dtype)
    for h in range(N_KV_HEADS):
        sl = slice(h * HEAD_DIM, (h + 1) * HEAD_DIM)
        ko_ref[:, sl] = _head_prep(kv_ref[:, sl], kg_ref[...], cos, sa, sb, True).astype(ko_ref.dtype)
    vo_ref[...] = kv_ref[:, KV_WIDTH:].astype(vo_ref.dtype)


def _qkv_prep(p, q_g, k_g, tables):
    tr = ROW_TILE
    rows = p.shape[0]
    cos, sin_a, sin_b = tables
    nt = cos.shape[0] // tr
    tab = pl.BlockSpec((tr, HEAD_DIM), lambda i: (i % nt, 0))
    vec = pl.BlockSpec((1, HEAD_DIM), lambda i: (0, 0))
    kv_out = pl.BlockSpec((tr, KV_WIDTH), lambda i: (i, 0))
    return pl.pallas_call(
        _qkv_prep_kernel,
        grid=(rows // tr,),
        in_specs=[pl.BlockSpec((tr, ATTN_WIDTH), lambda i: (i, 0)),
                  pl.BlockSpec((tr, 2 * KV_WIDTH), lambda i: (i, Q_END // (2 * KV_WIDTH))),
                  vec, vec, tab, tab, tab],
        out_specs=[pl.BlockSpec((tr, ATTN_WIDTH), lambda i: (i, 0)), kv_out, kv_out],
        out_shape=[jax.ShapeDtypeStruct((rows, ATTN_WIDTH), BF16),
                   jax.ShapeDtypeStruct((rows, KV_WIDTH), BF16),
                   jax.ShapeDtypeStruct((rows, KV_WIDTH), BF16)],
        compiler_params=_cparams(1),
        name="qkv_prep",
    )(p, p, q_g, k_g, cos, sin_a, sin_b)


def _kv_prep_ctx_kernel(kv_ref, kg_ref, ko_ref, vo_ref):
    for h in range(N_KV_HEADS):
        sl = slice(h * HEAD_DIM, (h + 1) * HEAD_DIM)
        ko_ref[:, sl] = _head_prep(kv_ref[:, sl], kg_ref[...], None, None, None, False).astype(ko_ref.dtype)
    vo_ref[...] = kv_ref[:, KV_WIDTH:].astype(vo_ref.dtype)


def _kv_prep_ctx(pc_kv, k_g):
    tr = ROW_TILE
    rows = pc_kv.shape[0]
    kv_out = pl.BlockSpec((tr, KV_WIDTH), lambda i: (i, 0))
    return pl.pallas_call(
        _kv_prep_ctx_kernel,
        grid=(rows // tr,),
        in_specs=[pl.BlockSpec((tr, 2 * KV_WIDTH), lambda i: (i, 0)),
                  pl.BlockSpec((1, HEAD_DIM), lambda i: (0, 0))],
        out_specs=[kv_out, kv_out],
        out_shape=[jax.ShapeDtypeStruct((rows, KV_WIDTH), BF16)] * 2,
        compiler_params=_cparams(1),
        name="kv_prep_ctx",
    )(pc_kv, k_g)


def _attn_kernel(q_ref, k_ref, v_ref, o_ref):
    k = k_ref[...]
    v = v_ref[...]
    for g in range(Q_PER_KV):
        sl = slice(g * HEAD_DIM, (g + 1) * HEAD_DIM)
        s = lax.dot_general(q_ref[:, sl], k, (((1,), (1,)), ((), ())), preferred_element_type=F32)
        m = jnp.max(s, axis=-1, keepdims=True)
        p = jnp.exp2(s - m)
        l = jnp.sum(p, axis=-1, keepdims=True)
        o = jnp.dot(p.astype(BF16), v, preferred_element_type=F32)
        o_ref[:, sl] = (o / l).astype(o_ref.dtype)


def _attention(q, k_all, v_all, n_lat):
    tq = 512
    b, t, _ = k_all.shape
    nq = n_lat // tq
    gw = Q_PER_KV * HEAD_DIM
    return pl.pallas_call(
        _attn_kernel,
        grid=(b, N_KV_HEADS, nq),
        in_specs=[pl.BlockSpec((tq, gw), lambda bi, h, i: (bi * nq + i, h)),
                  pl.BlockSpec((None, t, HEAD_DIM), lambda bi, h, i: (bi, 0, h)),
                  pl.BlockSpec((None, t, HEAD_DIM), lambda bi, h, i: (bi, 0, h))],
        out_specs=pl.BlockSpec((tq, gw), lambda bi, h, i: (bi * nq + i, h)),
        out_shape=jax.ShapeDtypeStruct(q.shape, BF16),
        compiler_params=_cparams(3),
        name="attention",
    )(q, k_all, v_all)


def _filter_kernel(z_ref, w1_ref, b1_ref, w2_ref, b2_ref, w3_ref, b3_ref, fr_ref, wo_ref, t_ref, ad_ref, o_ref,
                   a_ref):
    j = pl.program_id(0)
    hi = lax.Precision.HIGHEST

    @pl.when(j == 0)
    def _():
        fr = fr_ref[...]
        a = jnp.sin(fr * (jnp.dot(z_ref[...], w1_ref[...], precision=hi, preferred_element_type=F32)
                          + b1_ref[...]))
        a = jnp.sin(fr * (jnp.dot(a, w2_ref[...], precision=hi, preferred_element_type=F32) + b2_ref[...]))
        a_ref[...] = jnp.sin(fr * (jnp.dot(a, w3_ref[...], precision=hi, preferred_element_type=F32)
                                   + b3_ref[...]))

    h = jnp.dot(a_ref[...], wo_ref[...], precision=hi, preferred_element_type=F32)
    h = h * jnp.exp(-t_ref[...] * ad_ref[...])
    backward = (j // 2) % 2 == 1
    row = lax.broadcasted_iota(jnp.int32, h.shape, 0)
    o_ref[...] = jnp.where(jnp.logical_and(backward, row == 0), 0.0, h)


def _pad2(a, rows, cols):
    return jnp.pad(a, ((0, rows - a.shape[0]), (0, cols - a.shape[1])))


def _hyena_filters(n, w1, b1, w2, b2, w3, b3, freq, w_out):
    t = jnp.linspace(0.0, 1.0, n, dtype=F32)[:, None]
    bands = (FILTER_EMB - 1) // 2
    w = 2.0 * math.pi * jnp.arange(n, dtype=F32)[:, None] / n
    f = jnp.linspace(1e-4, bands - 1, bands, dtype=F32)[None, :]
    z = jnp.concatenate([t, jnp.cos(f * w), -jnp.sin(f * w)], axis=-1)
    max_decay = math.log(DECAY_TARGET) / FAST_DECAY_PCT
    min_decay = math.log(DECAY_TARGET) / SLOW_DECAY_PCT
    ad = jnp.abs(jnp.linspace(min_decay, max_decay, HYENA_WIDTH, dtype=F32))[None, :]
    p = FILT_PAD
    tn = 1024
    n_out = w_out.shape[1]
    full = lambda shape: pl.BlockSpec(shape, lambda j: (0, 0))
    return pl.pallas_call(
        _filter_kernel,
        grid=(n_out // tn,),
        in_specs=[full((n, p)), full((p, p)), full((1, p)), full((p, p)), full((1, p)), full((p, p)),
                  full((1, p)), full((1, p)),
                  pl.BlockSpec((p, tn), lambda j: (0, j)),
                  full((n, 1)),
                  pl.BlockSpec((1, tn), lambda j: (0, j % (HYENA_WIDTH // tn)))],
        out_specs=pl.BlockSpec((n, tn), lambda j: (0, j)),
        out_shape=jax.ShapeDtypeStruct((n, n_out), F32),
        scratch_shapes=[pltpu.VMEM((n, p), F32)],
        compiler_params=_cparams(1),
        name="hyena_filters",
    )(_pad2(z, n, p), _pad2(w1, p, p), _pad2(b1[None, :], 1, p), _pad2(w2, p, p), _pad2(b2[None, :], 1, p),
      _pad2(w3, p, p), _pad2(b3[None, :], 1, p), _pad2(freq[None, :], 1, p), _pad2(w_out, p, n_out), t, ad)


def _dft_matrix_kernel(ca_ref, sa_ref, cb_ref, sb_ref, alt_ref, cs_ref, fre_ref, fim_ref, gre_ref, gim_ref):
    i = pl.program_id(0)
    ca, sa = ca_ref[...], sa_ref[...]
    cb, sb = cb_ref[...], sb_ref[...]
    cos = ca * cb - sa * sb
    sin = sa * cb + ca * sb
    cs = cs_ref[...]
    row = lax.broadcasted_iota(jnp.int32, cos.shape, 0)
    col = lax.broadcasted_iota(jnp.int32, cos.shape, 1)
    nyquist_row = jnp.logical_and(i == 0, row == 0)
    alt_rows = jnp.where(row % 2 == 0, 1.0, -1.0)
    fre_ref[...] = cos.astype(fre_ref.dtype)
    fim_ref[...] = jnp.where(nyquist_row, alt_ref[...], -sin).astype(fim_ref.dtype)
    gre_ref[...] = (cos * cs).astype(gre_ref.dtype)
    gim_ref[...] = (jnp.where(col == 0, alt_rows, -sin) * cs).astype(gim_ref.dtype)


def _dft_matrices(n):
    big = 2 * n
    rb = DFT_ROWS
    t = jnp.arange(n, dtype=jnp.int32)[None, :]
    unit = 2.0 * math.pi / big
    coarse = ((rb * jnp.arange(n // rb, dtype=jnp.int32)[:, None] * t) % big).astype(F32) * unit
    fine = ((jnp.arange(rb, dtype=jnp.int32)[:, None] * t) % big).astype(F32) * unit
    alt = jnp.where(t % 2 == 0, 1.0, -1.0).astype(F32)
    cs = jnp.where(t == 0, 1.0 / big, 2.0 / big).astype(F32)
    coarse_spec = pl.BlockSpec((None, 1, n), lambda i: (i, 0, 0))
    whole = lambda r: pl.BlockSpec((r, n), lambda i: (0, 0))
    out_spec = pl.BlockSpec((rb, n), lambda i: (i, 0))
    return pl.pallas_call(
        _dft_matrix_kernel,
        grid=(n // rb,),
        in_specs=[coarse_spec, coarse_spec, whole(rb), whole(rb), whole(1), whole(1)],
        out_specs=[out_spec] * 4,
        out_shape=[jax.ShapeDtypeStruct((n, n), BF16)] * 4,
        compiler_params=_cparams(1),
        name="dft_matrices",
    )(jnp.cos(coarse)[:, None, :], jnp.sin(coarse)[:, None, :], jnp.cos(fine), jnp.sin(fine), alt, cs)


def _spectrum_kernel(fre_ref, fim_ref, hf_ref, hb_ref, kre_ref, kim_ref):
    i = pl.program_id(0)
    hf = hf_ref[...]
    hb = hb_ref[...]
    h_sum = (hf + hb).astype(BF16)
    h_dif = (hf - hb).astype(BF16)
    kre_ref[...] = jnp.dot(fre_ref[...], h_sum, preferred_element_type=F32)
    kim = jnp.dot(fim_ref[...], h_dif, preferred_element_type=F32)
    nyq = jnp.dot(fim_ref[pl.ds(0, BF16_SUBLANES), :], h_sum, preferred_element_type=F32)[0:1, :]
    row = lax.broadcasted_iota(jnp.int32, kim.shape, 0)
    kim_ref[...] = jnp.where(jnp.logical_and(i == 0, row == 0), nyq, kim)


def _filter_spectra(h_time, f_re, f_im):
    n = h_time.shape[0]
    tf, tc = 1024, 512
    per = HYENA_WIDTH // tc
    return pl.pallas_call(
        _spectrum_kernel,
        grid=(n // tf, 2 * per),
        in_specs=[pl.BlockSpec((tf, n), lambda i, j: (i, 0)),
                  pl.BlockSpec((tf, n), lambda i, j: (i, 0)),
                  pl.BlockSpec((n, tc), lambda i, j: (0, (j // per) * 2 * per + j % per)),
                  pl.BlockSpec((n, tc), lambda i, j: (0, (j // per) * 2 * per + per + j % per))],
        out_specs=[pl.BlockSpec((tf, tc), lambda i, j: (i, j))] * 2,
        out_shape=[jax.ShapeDtypeStruct((n, 2 * HYENA_WIDTH), F32)] * 2,
        compiler_params=_cparams(2),
        name="filter_spectra",
    )(f_re, f_im, h_time, h_time)


def _short_conv_kernel(p_ref, w_ref, b_ref, u_ref, v16_ref):
    j = pl.program_id(1)
    x = p_ref[...]
    n = x.shape[0]
    row = lax.broadcasted_iota(jnp.int32, x.shape, 0)
    prev = jnp.where(row == 0, 0.0, pltpu.roll(x, 1, axis=0))
    nxt = jnp.where(row == n - 1, 0.0, pltpu.roll(x, n - 1, axis=0))
    u = prev * w_ref[0:1, :] + x * w_ref[1:2, :] + nxt * w_ref[2:3, :] + b_ref[...]
    u_ref[...] = u

    @pl.when(j < HYENA_WIDTH // u.shape[1])
    def _():
        v16_ref[...] = u.astype(v16_ref.dtype)


def _short_conv(p3, w, b):
    bsz, n, _ = p3.shape
    tc = 512
    nj = 3 * HYENA_WIDTH // tc
    nv = HYENA_WIDTH // tc
    off = V_END // tc
    return pl.pallas_call(
        _short_conv_kernel,
        grid=(bsz, nj),
        in_specs=[pl.BlockSpec((None, n, tc), lambda bi, j: (bi, 0, off + j)),
                  pl.BlockSpec((3, tc), lambda bi, j: (0, j)),
                  pl.BlockSpec((1, tc), lambda bi, j: (0, j))],
        out_specs=[pl.BlockSpec((None, n, tc), lambda bi, j: (bi, 0, j)),
                   pl.BlockSpec((None, n, tc), lambda bi, j: (bi, 0, jnp.minimum(j, nv - 1)))],
        out_shape=[jax.ShapeDtypeStruct((bsz, n, 3 * HYENA_WIDTH), F32),
                   jax.ShapeDtypeStruct((bsz, n, HYENA_WIDTH), BF16)],
        compiler_params=_cparams(2),
        name="short_conv",
    )(p3, w, b)


def _dft_fwd_kernel(fre_ref, fim_ref, u_ref, kre_ref, kim_ref, yre_ref, yim_ref):
    f = pl.program_id(2)
    u = u_ref[...]
    ure = jnp.dot(fre_ref[...], u, preferred_element_type=F32)
    uim = jnp.dot(fim_ref[...], u, preferred_element_type=F32)
    kre = kre_ref[...]
    kim = kim_ref[...]
    row = lax.broadcasted_iota(jnp.int32, ure.shape, 0)
    real_pair = jnp.logical_and(f == 0, row == 0)
    yre_ref[...] = jnp.where(real_pair, ure * kre, ure * kre - uim * kim).astype(yre_ref.dtype)
    yim_ref[...] = jnp.where(real_pair, uim * kim, ure * kim + uim * kre).astype(yim_ref.dtype)


def _dft_forward(u16, f_re, f_im, k_re, k_im, order):
    bsz, n, ch = u16.shape
    tf, tc = 1024, 512
    nc = ch // tc
    return pl.pallas_call(
        _dft_fwd_kernel,
        grid=(bsz, nc, n // tf),
        in_specs=[pl.BlockSpec((tf, n), lambda bi, c, f: (f, 0)),
                  pl.BlockSpec((tf, n), lambda bi, c, f: (f, 0)),
                  pl.BlockSpec((None, n, tc), lambda bi, c, f: (bi, 0, c)),
                  pl.BlockSpec((tf, tc), lambda bi, c, f: (f, order * nc + c)),
                  pl.BlockSpec((tf, tc), lambda bi, c, f: (f, order * nc + c))],
        out_specs=[pl.BlockSpec((None, tf, tc), lambda bi, c, f: (bi, f, c))] * 2,
        out_shape=[jax.ShapeDtypeStruct((bsz, n, ch), BF16)] * 2,
        compiler_params=_cparams(3),
        name="dft_forward",
    )(f_re, f_im, u16, k_re, k_im)


def _dft_inv_kernel(gre_ref, gim_ref, yre_ref, yim_ref, u_ref, gate_ref, bias_ref, *o_refs):
    y = (jnp.dot(gre_ref[...], yre_ref[...], preferred_element_type=F32)
         + jnp.dot(gim_ref[...], yim_ref[...], preferred_element_type=F32))
    z = gate_ref[...] * (y + u_ref[...] * bias_ref[...])
    for o in o_refs:
        o[...] = z.astype(o.dtype)


def _dft_inverse(y_re, y_im, g_re, g_im, u_arr, u_blk0, gate_arr, gate_blk0, bias, out_dtypes):
    bsz, n, ch = y_re.shape
    tt, tc = 1024, 512
    nc = ch // tc
    return pl.pallas_call(
        _dft_inv_kernel,
        grid=(bsz, nc, n // tt),
        in_specs=[pl.BlockSpec((tt, n), lambda bi, c, t: (t, 0)),
                  pl.BlockSpec((tt, n), lambda bi, c, t: (t, 0)),
                  pl.BlockSpec((None, n, tc), lambda bi, c, t: (bi, 0, c)),
                  pl.BlockSpec((None, n, tc), lambda bi, c, t: (bi, 0, c)),
                  pl.BlockSpec((None, tt, tc), lambda bi, c, t: (bi, t, u_blk0 * nc + c)),
                  pl.BlockSpec((None, tt, tc), lambda bi, c, t: (bi, t, gate_blk0 * nc + c)),
                  pl.BlockSpec((1, tc), lambda bi, c, t: (0, c))],
        out_specs=[pl.BlockSpec((None, tt, tc), lambda bi, c, t: (bi, t, c))] * len(out_dtypes),
        out_shape=[jax.ShapeDtypeStruct((bsz, n, ch), dt) for dt in out_dtypes],
        compiler_params=_cparams(3),
        name="dft_inverse",
    )(g_re, g_im, y_re, y_im, u_arr, gate_arr, bias)


def _ffn(h, w_gate, w_up, w_down, lead, rows, side=None, side_args=()):
    n_tiles = pl.cdiv(D_FF, MM_TN)
    outs = _matmul_wres([h], [w_gate, w_up], [(0, 0), (0, 1)], _ep_swiglu, D_FF, [BF16], tm=MM_TM, tn=MM_TN,
                        m_tiles=rows // MM_TM, n_tiles=n_tiles, k=D_MODEL, ch=MM_CHUNK, lead=[lead, lead],
                        last_width=D_FF - (n_tiles - 1) * MM_TN, side=side, side_args=side_args,
                        name="ffn_gate_up")
    f = _matmul_wres([outs[0]], [w_down], [(0, 0)], _ep_store, D_MODEL, [BRANCH_DTYPE], tm=DOWN_TM, tn=MM_TN,
                     m_tiles=rows // DOWN_TM, n_tiles=D_MODEL // MM_TN, k=D_FF, ch=DOWN_CHUNK, lead=[lead],
                     name="ffn_down")[0]
    return f, (outs[1] if side is not None else None)


def kernel(x, c, ctx, c_ctx, w_ada, b_ada, pre_g, post_g, ffn_w_gate, ffn_w_up, ffn_w_down, w_in, q_norm_g,
           k_norm_g, short_w, short_b, filt_w1, filt_b1, filt_w2, filt_b2, filt_w3, filt_b3, filt_freq,
           filt_w_out, hyena_bias, w_br_attn, w_br_hyena, w_out):
    bsz, n_lat, d = x.shape
    n_ctx = ctx.shape[1]
    rows_x, rows_c = bsz * n_lat, bsz * n_ctx
    x2d = x.reshape(rows_x, d)
    c2d = ctx.reshape(rows_c, d)
    l = 0

    cond = jnp.concatenate([c, c_ctx[None, :], jnp.zeros((3, d), F32)], axis=0)
    n_first = 2 * d
    bias = b_ada[l][None, :]
    mods_first = _ada(cond, w_ada, bias, l, n_first)
    pre = [pre_g[l, s][None, :] for s in range(3)]
    post = [post_g[l, s][None, :] for s in range(3)]

    h0 = _modulate_first(x2d, c2d, mods_first[:5].reshape(5, 2, d), pre[0], shift_i=0, scale_i=1)
    side = _SidePlan(lead=(l,), k=d, ts=ADA_SIDE_TILE, n_tiles=(N_MOD * d - n_first) // ADA_SIDE_TILE,
                     col0=n_first)
    f1, mods_rest = _ffn(h0, ffn_w_gate, ffn_w_up, ffn_w_down, (l, 0), rows_x + rows_c, side=side,
                         side_args=(cond, w_ada, bias[:, n_first:]))
    mods = jnp.concatenate([mods_first, mods_rest], axis=1)[:5].reshape(5, N_MOD, d)
    x1, h1 = _residual(f1, x2d, c2d, mods, post[0], pre[1], gate_i=2, weight=MACARON_W, shift_i=3, scale_i=4)

    tm, tn = MM_TM, MM_TN
    p = _matmul_wres([h1], [w_in], [(0, 0)], _ep_store, IN_COLS, [F32], tm=tm, tn=IN_TN, m_tiles=rows_x // tm,
                     n_tiles=IN_COLS // IN_TN, k=d, ch=IN_CHUNK, lead=[(l,)], name="in_proj")[0]
    pc_kv = _matmul_wres([h1], [w_in], [(0, 0)], _ep_store, 2 * KV_WIDTH, [F32], tm=CTX_TM, tn=tn,
                         m_tiles=rows_c // CTX_TM, n_tiles=2 * KV_WIDTH // tn, k=d, ch=MM_CHUNK, lead=[(l,)],
                         col0=Q_END, a_blk0=rows_x // CTX_TM, name="in_proj_ctx")[0]

    tables = _rope_tables(n_lat)
    q16, k16, v16 = _qkv_prep(p, q_norm_g[l][None, :], k_norm_g[l][None, :], tables)
    kc16, vc16 = _kv_prep_ctx(pc_kv, k_norm_g[l][None, :])
    k_all = jnp.concatenate([kc16.reshape(bsz, n_ctx, KV_WIDTH), k16.reshape(bsz, n_lat, KV_WIDTH)], axis=1)
    v_all = jnp.concatenate([vc16.reshape(bsz, n_ctx, KV_WIDTH), v16.reshape(bsz, n_lat, KV_WIDTH)], axis=1)
    attn_o = _attention(q16, k_all, v_all, n_lat)

    f_re, f_im, g_re, g_im = _dft_matrices(n_lat)
    h_time = _hyena_filters(n_lat, filt_w1[l], filt_b1[l], filt_w2[l], filt_b2[l], filt_w3[l], filt_b3[l],
                            filt_freq[l], filt_w_out[l])
    k_re, k_im = _filter_spectra(h_time, f_re, f_im)
    p3 = p.reshape(bsz, n_lat, IN_COLS)
    u, v16h = _short_conv(p3, short_w[l], short_b[l][None, :])
    y_re, y_im = _dft_forward(v16h, f_re, f_im, k_re, k_im, order=0)
    z, z16 = _dft_inverse(y_re, y_im, g_re, g_im, u, 0, u, 1, hyena_bias[l, 0][None, :], [F32, BF16])
    y_re, y_im = _dft_forward(z16, f_re, f_im, k_re, k_im, order=1)
    hy_o = _dft_inverse(y_re, y_im, g_re, g_im, z, 0, u, 2, hyena_bias[l, 1][None, :], [BF16])[0]

    gate_spec = lambda off: pl.BlockSpec((tm, tn), lambda j, i: (i, off + j))
    merged = _matmul_wres([attn_o, hy_o.reshape(rows_x, HYENA_WIDTH)], [w_br_attn, w_br_hyena], [(0, 0), (1, 1)],
                          _ep_merge, d, [BF16], tm=tm, tn=tn, m_tiles=rows_x // tm, n_tiles=d // tn,
                          k=ATTN_WIDTH, ch=MM_CHUNK, lead=[(l,), (l,)], extras=(p, p),
                          extra_specs=(gate_spec(HY_END // tn), gate_spec((HY_END + d) // tn)), name="merge")[0]
    out = _matmul_wres([merged], [w_out], [(0, 0)], _ep_store, d, [BRANCH_DTYPE], tm=tm, tn=IN_TN,
                       m_tiles=rows_x // tm, n_tiles=d // IN_TN, k=d, ch=IN_CHUNK, lead=[(l,)], name="out_proj")[0]
    x2, h2 = _residual(out, x1, None, mods, post[1], pre[2], gate_i=5, weight=1.0, shift_i=6, scale_i=7)

    f2, _ = _ffn(h2, ffn_w_gate, ffn_w_up, ffn_w_down, (l, 1), rows_x)
    x3 = _residual(f2, x2, None, mods, post[2], None, gate_i=8, weight=MACARON_W)[0]
    return x3.reshape(bsz, n_lat, d)
```

```python
import functools
import math
from typing import NamedTuple

import jax
import jax.numpy as jnp
from jax import lax
from jax.experimental import pallas as pl
from jax.experimental.pallas import tpu as pltpu

F32 = jnp.float32
BF16 = jnp.bfloat16
BRANCH_DTYPE = BF16

D_MODEL = 4096
GRID_W = 64
HEAD_DIM = 128
N_Q_HEADS = 16
N_KV_HEADS = 4
Q_PER_KV = N_Q_HEADS // N_KV_HEADS
ATTN_WIDTH = N_Q_HEADS * HEAD_DIM
KV_WIDTH = N_KV_HEADS * HEAD_DIM
ROPE_THETA = 10000.0
AXIS_DIM = HEAD_DIM // 2
HYENA_WIDTH = D_MODEL // 2
FILTER_EMB = 33
FILTER_HIDDEN = 64
DECAY_TARGET = 1e-2
FAST_DECAY_PCT = 0.3
SLOW_DECAY_PCT = 1.5
D_FF = 11008
MACARON_W = 0.5
N_MOD = 9
NORM_EPS = 1e-6
Q_END = ATTN_WIDTH
K_END = Q_END + KV_WIDTH
V_END = K_END + KV_WIDTH
HY_END = V_END + 3 * HYENA_WIDTH
IN_COLS = HY_END + 2 * D_MODEL
SCORE_SCALE_LOG2 = math.log2(math.e) / math.sqrt(HEAD_DIM)

V7X_LANES = 128
BF16_SUBLANES = 16
V7X_VMEM_LIMIT_BYTES = 56 * 1024 * 1024

MM_TM = 1024
MM_TN = 512
MM_CHUNK = 1024
IN_TN = 1024
IN_CHUNK = 512
CTX_TM = 256
DOWN_TM = 512
DOWN_CHUNK = D_FF // 16
ROW_TILE = 256
ADA_SIDE_TILE = 256
CONV_BLOCK = 1024
DFT_ROWS = 64
FILT_PAD = V7X_LANES


def _cparams(n_axes):
    return pltpu.CompilerParams(dimension_semantics=("arbitrary",) * n_axes,
                                vmem_limit_bytes=V7X_VMEM_LIMIT_BYTES)


def _rms(x):
    return x * lax.rsqrt(jnp.mean(x * x, axis=-1, keepdims=True) + NORM_EPS)


def _ada_kernel(c_ref, w_ref, b_ref, o_ref):
    c = c_ref[...]
    s = (c * jax.nn.sigmoid(c)).astype(BF16)
    o_ref[...] = jnp.dot(s, w_ref[...].astype(BF16), preferred_element_type=F32) + b_ref[...]


def _ada(cond, w, b, layer, n):
    tn = 512
    return pl.pallas_call(
        _ada_kernel,
        grid=(n // tn,),
        in_specs=[pl.BlockSpec((8, D_MODEL), lambda j: (0, 0)),
                  pl.BlockSpec((None, D_MODEL, tn), lambda j: (layer, 0, j)),
                  pl.BlockSpec((1, tn), lambda j: (0, j))],
        out_specs=pl.BlockSpec((8, tn), lambda j: (0, j)),
        out_shape=jax.ShapeDtypeStruct((8, n), F32),
        compiler_params=_cparams(1),
        name="ada",
    )(cond, w, b)


def _mod_index(i, rows_per_batch_tiles):
    return jnp.minimum(i // rows_per_batch_tiles, 4)


def _modulate_kernel(x_ref, c_ref, m_ref, g_ref, o_ref, *, nx, shift_i, scale_i):
    i = pl.program_id(0)

    def body(src):
        y = _rms(src[...]) * g_ref[...]
        o_ref[...] = (y * (1.0 + m_ref[0, scale_i:scale_i + 1, :])
                      + m_ref[0, shift_i:shift_i + 1, :]).astype(o_ref.dtype)

    pl.when(i < nx)(lambda: body(x_ref))
    pl.when(i >= nx)(lambda: body(c_ref))


def _modulate_first(x2d, c2d, mods, g, shift_i, scale_i):
    tr = ROW_TILE
    nx, nc = x2d.shape[0] // tr, c2d.shape[0] // tr
    per_batch = (x2d.shape[0] // 4) // tr
    return pl.pallas_call(
        functools.partial(_modulate_kernel, nx=nx, shift_i=shift_i, scale_i=scale_i),
        grid=(nx + nc,),
        in_specs=[pl.BlockSpec((tr, D_MODEL), lambda i: (jnp.minimum(i, nx - 1), 0)),
                  pl.BlockSpec((tr, D_MODEL), lambda i: (jnp.maximum(i - nx, 0), 0)),
                  pl.BlockSpec((1, mods.shape[1], D_MODEL), lambda i: (_mod_index(i, per_batch), 0, 0)),
                  pl.BlockSpec((1, D_MODEL), lambda i: (0, 0))],
        out_specs=pl.BlockSpec((tr, D_MODEL), lambda i: (i, 0)),
        out_shape=jax.ShapeDtypeStruct(((nx + nc) * tr, D_MODEL), BF16),
        compiler_params=_cparams(1),
        name="modulate_first",
    )(x2d, c2d, mods, g)


def _resid_kernel(*refs, nx, two_source, gate_i, weight, shift_i, scale_i, with_next):
    it = iter(refs)
    f_ref = next(it)
    x_ref = next(it)
    c_ref = next(it) if two_source else None
    m_ref = next(it)
    pg_ref = next(it)
    ng_ref = next(it) if with_next else None
    xo_ref = next(it)
    ho_ref = next(it) if with_next else None
    i = pl.program_id(0)

    def body(src):
        y = _rms(f_ref[...].astype(F32)) * pg_ref[...]
        xn = src[...] + (weight * m_ref[0, gate_i:gate_i + 1, :]) * y
        xo_ref[...] = xn
        if with_next:
            h = _rms(xn) * ng_ref[...]
            ho_ref[...] = (h * (1.0 + m_ref[0, scale_i:scale_i + 1, :])
                           + m_ref[0, shift_i:shift_i + 1, :]).astype(ho_ref.dtype)

    if two_source:
        pl.when(i < nx)(lambda: body(x_ref))
        pl.when(i >= nx)(lambda: body(c_ref))
    else:
        body(x_ref)


def _residual(f, x2d, c2d, mods, post_g, next_g, *, gate_i, weight, shift_i=0, scale_i=0):
    tr = ROW_TILE
    rows = f.shape[0]
    n = rows // tr
    two_source = c2d is not None
    with_next = next_g is not None
    nx = x2d.shape[0] // tr if two_source else n
    per_batch = (8192 // 4) // tr
    row = pl.BlockSpec((tr, D_MODEL), lambda i: (i, 0))
    vec = pl.BlockSpec((1, D_MODEL), lambda i: (0, 0))
    in_specs = [row]
    args = [f]
    if two_source:
        in_specs += [pl.BlockSpec((tr, D_MODEL), lambda i: (jnp.minimum(i, nx - 1), 0)),
                     pl.BlockSpec((tr, D_MODEL), lambda i: (jnp.maximum(i - nx, 0), 0))]
        args += [x2d, c2d]
    else:
        in_specs += [row]
        args += [x2d]
    in_specs += [pl.BlockSpec((1, N_MOD, D_MODEL), lambda i: (_mod_index(i, per_batch), 0, 0)), vec]
    args += [mods, post_g]
    out_specs = [row]
    out_shape = [jax.ShapeDtypeStruct((rows, D_MODEL), F32)]
    if with_next:
        in_specs += [vec]
        args += [next_g]
        out_specs += [row]
        out_shape += [jax.ShapeDtypeStruct((rows, D_MODEL), BF16)]
    return pl.pallas_call(
        functools.partial(_resid_kernel, nx=nx, two_source=two_source, gate_i=gate_i, weight=weight,
                          shift_i=shift_i, scale_i=scale_i, with_next=with_next),
        grid=(n,),
        in_specs=in_specs,
        out_specs=out_specs,
        out_shape=out_shape,
        compiler_params=_cparams(1),
        name="residual",
    )(*args)


class _WeightPlan(NamedTuple):
    k: int
    ch: int
    tn: int
    n_tiles: int
    m_tiles: int
    col0: int
    last_width: int
    lead: tuple


def _wres_kernel(*refs, n_a, n_w, pairs, n_extra, n_out, epilogue, plan, side):
    it = iter(refs)
    a_refs = [next(it) for _ in range(n_a)]
    w_refs = [next(it) for _ in range(n_w)]
    e_refs = [next(it) for _ in range(n_extra)]
    if side is not None:
        side_c_ref, side_w_ref, side_b_ref = next(it), next(it), next(it)
    o_refs = [next(it) for _ in range(n_out)]
    if side is not None:
        side_o_ref = next(it)
    wbf, stage, sem = next(it), next(it), next(it)
    if side is not None:
        side_stage, side_sem = next(it), next(it)
    j = pl.program_id(0)
    i = pl.program_id(1)
    slot = j % 2
    per_w = plan.k // plan.ch
    n_chunks = n_w * per_w
    ragged = plan.last_width != plan.tn

    def chunk_copy(tile, c, width):
        w, r = divmod(c, per_w)
        col = plan.col0 + tile * plan.tn
        if not isinstance(col, int):
            col = pl.multiple_of(col, V7X_LANES)
        src = w_refs[w].at[(*plan.lead[w], pl.ds(r * plan.ch, plan.ch), pl.ds(col, width))]
        return pltpu.make_async_copy(src, stage.at[c % 2, :, pl.ds(0, width)], sem.at[c % 2])

    def width_variants(tile):
        if not ragged:
            return [(plan.tn, tile < plan.n_tiles)]
        return [(plan.tn, tile < plan.n_tiles - 1), (plan.last_width, tile == plan.n_tiles - 1)]

    @pl.when(jnp.logical_and(j == 0, i == 0))
    def _():
        chunk_copy(0, 0, plan.tn).start()
        for c in range(n_chunks):
            w, r = divmod(c, per_w)
            chunk_copy(0, c, plan.tn).wait()
            if c + 1 < n_chunks:
                chunk_copy(0, c + 1, plan.tn).start()
            wbf[0, w, pl.ds(r * plan.ch, plan.ch), :] = stage[c % 2].astype(BF16)
        second = plan.last_width if plan.n_tiles == 2 else plan.tn
        chunk_copy(1, 0, second).start()

    for width, cond in width_variants(j + 1):
        for c in range(n_chunks):
            @pl.when(jnp.logical_and(cond, i == c))
            def _(width=width, c=c):
                chunk_copy(j + 1, c, width).wait()
                if c + 1 < n_chunks:
                    chunk_copy(j + 1, c + 1, width).start()

    for width, cond in width_variants(j + 2):
        @pl.when(jnp.logical_and(cond, i == plan.m_tiles - 1))
        def _(width=width):
            chunk_copy(j + 2, 0, width).start()

    if side is not None:
        _side_stream_dma(side, side_w_ref, side_stage, side_sem, j * plan.m_tiles + i)

    def compute(width):
        c_now = jnp.minimum(i, n_chunks - 1)
        w_now = c_now // per_w
        row_now = pl.multiple_of((c_now % per_w) * plan.ch, BF16_SUBLANES)
        wbf[1 - slot, w_now, pl.ds(row_now, plan.ch), :] = stage[c_now % 2].astype(BF16)
        if side is not None:
            _side_stream_compute(side, side_c_ref, side_b_ref, side_o_ref, side_stage, j * plan.m_tiles + i)
        prods = [jnp.dot(a_refs[ai][...], wbf[slot, wi, :, pl.ds(0, width)], preferred_element_type=F32)
                 for ai, wi in pairs]
        epilogue(prods, e_refs, o_refs, width)

    if ragged:
        pl.when(j < plan.n_tiles - 1)(lambda: compute(plan.tn))
        pl.when(j == plan.n_tiles - 1)(lambda: compute(plan.last_width))
    else:
        compute(plan.tn)


class _SidePlan(NamedTuple):
    lead: tuple
    k: int
    ts: int
    n_tiles: int
    col0: int


def _side_stream_dma(side, w_ref, stage, sem, g):
    def copy(tile, slot):
        col = pl.multiple_of(side.col0 + tile * side.ts, V7X_LANES)
        return pltpu.make_async_copy(w_ref.at[(*side.lead, slice(None), pl.ds(col, side.ts))],
                                     stage.at[slot], sem.at[slot])

    pl.when(g == 0)(lambda: copy(g, 0).start())
    for slot in range(2):
        pl.when(jnp.logical_and(g < side.n_tiles, g % 2 == slot))(lambda slot=slot: copy(g, slot).wait())
    for slot in range(2):
        pl.when(jnp.logical_and(g + 1 < side.n_tiles, (g + 1) % 2 == slot))(
            lambda slot=slot: copy(g + 1, slot).start())


def _side_stream_compute(side, c_ref, b_ref, o_ref, stage, g):
    t = jnp.minimum(g, side.n_tiles - 1)
    c = c_ref[...]
    s = (c * jax.nn.sigmoid(c)).astype(BF16)
    o_ref[...] = jnp.dot(s, stage[t % 2].astype(BF16), preferred_element_type=F32) + b_ref[...]


def _matmul_wres(a_list, w_list, pairs, epilogue, out_cols, out_dtypes, *, tm, tn, m_tiles, n_tiles, k, ch,
                 lead, col0=0, last_width=None, a_blk0=0, extras=(), extra_specs=(), side=None, side_args=(),
                 name="matmul"):
    plan = _WeightPlan(k=k, ch=ch, tn=tn, n_tiles=n_tiles, m_tiles=m_tiles, col0=col0,
                       last_width=last_width or tn, lead=tuple(lead))
    n_chunks = len(w_list) * (k // ch)
    assert n_chunks <= m_tiles and n_chunks % 2 == 0 and n_tiles >= 2 and k % ch == 0
    in_specs = [pl.BlockSpec((tm, k), lambda j, i: (i + a_blk0, 0)) for _ in a_list]
    in_specs += [pl.BlockSpec(memory_space=pl.ANY) for _ in w_list]
    in_specs += list(extra_specs)
    out_specs = [pl.BlockSpec((tm, tn), lambda j, i: (i, j)) for _ in out_dtypes]
    out_shape = [jax.ShapeDtypeStruct((m_tiles * tm, out_cols), dt) for dt in out_dtypes]
    scratch = [pltpu.VMEM((2, len(w_list), k, tn), BF16), pltpu.VMEM((2, ch, tn), F32),
               pltpu.SemaphoreType.DMA((2,))]
    if side is not None:
        assert side.n_tiles <= n_tiles * m_tiles
        side_tile = lambda j, i: (0, jnp.minimum(j * m_tiles + i, side.n_tiles - 1))
        in_specs += [pl.BlockSpec((8, side.k), lambda j, i: (0, 0)), pl.BlockSpec(memory_space=pl.ANY),
                     pl.BlockSpec((1, side.ts), side_tile)]
        out_specs += [pl.BlockSpec((8, side.ts), side_tile)]
        out_shape += [jax.ShapeDtypeStruct((8, side.n_tiles * side.ts), F32)]
        scratch += [pltpu.VMEM((2, side.k, side.ts), F32), pltpu.SemaphoreType.DMA((2,))]
    return pl.pallas_call(
        functools.partial(_wres_kernel, n_a=len(a_list), n_w=len(w_list), pairs=tuple(pairs),
                          n_extra=len(extras), n_out=len(out_dtypes), epilogue=epilogue, plan=plan, side=side),
        grid=(n_tiles, m_tiles),
        in_specs=in_specs,
        out_specs=out_specs,
        out_shape=out_shape,
        scratch_shapes=scratch,
        compiler_params=_cparams(2),
        name=name,
    )(*a_list, *w_list, *extras, *side_args)


def _ep_store(accs, e_refs, o_refs, width):
    o_refs[0][:, pl.ds(0, width)] = accs[0].astype(o_refs[0].dtype)


def _ep_swiglu(accs, e_refs, o_refs, width):
    g, u = accs
    o_refs[0][:, pl.ds(0, width)] = (g * jax.nn.sigmoid(g) * u).astype(o_refs[0].dtype)


def _ep_merge(accs, e_refs, o_refs, width):
    a, h = accs
    ga, gh = e_refs
    o_refs[0][...] = (jax.nn.sigmoid(ga[...]) * a + jax.nn.sigmoid(gh[...]) * h).astype(o_refs[0].dtype)


def _rope_tables(n_lat):
    t = jnp.arange(n_lat, dtype=jnp.int32)
    row = (t // GRID_W).astype(F32)
    col = (t % GRID_W).astype(F32)
    inv_freq = ROPE_THETA ** (-jnp.arange(0, AXIS_DIM, 2, dtype=F32) / AXIS_DIM)
    ang = jnp.concatenate([jnp.tile(row[:, None] * inv_freq, (1, 2)),
                           jnp.tile(col[:, None] * inv_freq, (1, 2))], axis=-1)
    lane = jnp.arange(HEAD_DIM, dtype=jnp.int32)
    first = (lane % AXIS_DIM) < (AXIS_DIM // 2)
    cos = jnp.cos(ang)
    sin = jnp.sin(ang)
    sin_a = jnp.where(first, -sin, 0.0)
    sin_b = jnp.where(first, 0.0, sin)
    return cos, sin_a, sin_b


def _head_prep(x, g, cos, sin_a, sin_b, rope):
    y = _rms(x) * g
    if rope:
        up = pltpu.roll(y, HEAD_DIM - AXIS_DIM // 2, axis=1)
        dn = pltpu.roll(y, AXIS_DIM // 2, axis=1)
        y = y * cos + up * sin_a + dn * sin_b
    return y


def _qkv_prep_kernel(q_ref, kv_ref, qg_ref, kg_ref, cos_ref, sa_ref, sb_ref, qo_ref, ko_ref, vo_ref):
    cos, sa, sb = cos_ref[...], sa_ref[...], sb_ref[...]
    for h in range(N_Q_HEADS):
        sl = slice(h * HEAD_DIM, (h + 1) * HEAD_DIM)
        q = _head_prep(q_ref[:, sl], qg_ref[...], cos, sa, sb, True)
        qo_ref[:, sl] = (q * SCORE_SCALE_LOG2).astype(qo_ref.dtype)
    for h in range(N_KV_HEADS):
        sl = slice(h * HEAD_DIM, (h + 1) * HEAD_DIM)
        ko_ref[:, sl] = _head_prep(kv_ref[:, sl], kg_ref[...], cos, sa, sb, True).astype(ko_ref.dtype)
    vo_ref[...] = kv_ref[:, KV_WIDTH:].astype(vo_ref.dtype)


def _qkv_prep(p, q_g, k_g, tables):
    tr = ROW_TILE
    rows = p.shape[0]
    cos, sin_a, sin_b = tables
    nt = cos.shape[0] // tr
    tab = pl.BlockSpec((tr, HEAD_DIM), lambda i: (i % nt, 0))
    vec = pl.BlockSpec((1, HEAD_DIM), lambda i: (0, 0))
    kv_out = pl.BlockSpec((tr, KV_WIDTH), lambda i: (i, 0))
    return pl.pallas_call(
        _qkv_prep_kernel,
        grid=(rows // tr,),
        in_specs=[pl.BlockSpec((tr, ATTN_WIDTH), lambda i: (i, 0)),
                  pl.BlockSpec((tr, 2 * KV_WIDTH), lambda i: (i, Q_END // (2 * KV_WIDTH))),
                  vec, vec, tab, tab, tab],
        out_specs=[pl.BlockSpec((tr, ATTN_WIDTH), lambda i: (i, 0)), kv_out, kv_out],
        out_shape=[jax.ShapeDtypeStruct((rows, ATTN_WIDTH), BF16),
                   jax.ShapeDtypeStruct((rows, KV_WIDTH), BF16),
                   jax.ShapeDtypeStruct((rows, KV_WIDTH), BF16)],
        compiler_params=_cparams(1),
        name="qkv_prep",
    )(p, p, q_g, k_g, cos, sin_a, sin_b)


def _kv_prep_ctx_kernel(kv_ref, kg_ref, ko_ref, vo_ref):
    for h in range(N_KV_HEADS):
        sl = slice(h * HEAD_DIM, (h + 1) * HEAD_DIM)
        ko_ref[:, sl] = _head_prep(kv_ref[:, sl], kg_ref[...], None, None, None, False).astype(ko_ref.dtype)
    vo_ref[...] = kv_ref[:, KV_WIDTH:].astype(vo_ref.dtype)


def _kv_prep_ctx(pc_kv, k_g):
    tr = ROW_TILE
    rows = pc_kv.shape[0]
    kv_out = pl.BlockSpec((tr, KV_WIDTH), lambda i: (i, 0))
    return pl.pallas_call(
        _kv_prep_ctx_kernel,
        grid=(rows // tr,),
        in_specs=[pl.BlockSpec((tr, 2 * KV_WIDTH), lambda i: (i, 0)),
                  pl.BlockSpec((1, HEAD_DIM), lambda i: (0, 0))],
        out_specs=[kv_out, kv_out],
        out_shape=[jax.ShapeDtypeStruct((rows, KV_WIDTH), BF16)] * 2,
        compiler_params=_cparams(1),
        name="kv_prep_ctx",
    )(pc_kv, k_g)


def _attn_kernel(q_ref, k_ref, v_ref, o_ref):
    k = k_ref[...]
    v = v_ref[...]
    for g in range(Q_PER_KV):
        sl = slice(g * HEAD_DIM, (g + 1) * HEAD_DIM)
        s = lax.dot_general(q_ref[:, sl], k, (((1,), (1,)), ((), ())), preferred_element_type=F32)
        m = jnp.max(s, axis=-1, keepdims=True)
        p = jnp.exp2(s - m)
        l = jnp.sum(p, axis=-1, keepdims=True)
        o = jnp.dot(p.astype(BF16), v, preferred_element_type=F32)
        o_ref[:, sl] = (o / l).astype(o_ref.dtype)


def _attention(q, k_all, v_all, n_lat):
    tq = 512
    b, t, _ = k_all.shape
    nq = n_lat // tq
    gw = Q_PER_KV * HEAD_DIM
    return pl.pallas_call(
        _attn_kernel,
        grid=(b, N_KV_HEADS, nq),
        in_specs=[pl.BlockSpec((tq, gw), lambda bi, h, i: (bi * nq + i, h)),
                  pl.BlockSpec((None, t, HEAD_DIM), lambda bi, h, i: (bi, 0, h)),
                  pl.BlockSpec((None, t, HEAD_DIM), lambda bi, h, i: (bi, 0, h))],
        out_specs=pl.BlockSpec((tq, gw), lambda bi, h, i: (bi * nq + i, h)),
        out_shape=jax.ShapeDtypeStruct(q.shape, BF16),
        compiler_params=_cparams(3),
        name="attention",
    )(q, k_all, v_all)


def _filter_kernel(z_ref, w1_ref, b1_ref, w2_ref, b2_ref, w3_ref, b3_ref, fr_ref, wo_ref, t_ref, ad_ref, o_ref,
                   a_ref):
    j = pl.program_id(0)
    hi = lax.Precision.HIGHEST

    @pl.when(j == 0)
    def _():
        fr = fr_ref[...]
        a = jnp.sin(fr * (jnp.dot(z_ref[...], w1_ref[...], precision=hi, preferred_element_type=F32)
                          + b1_ref[...]))
        a = jnp.sin(fr * (jnp.dot(a, w2_ref[...], precision=hi, preferred_element_type=F32) + b2_ref[...]))
        a_ref[...] = jnp.sin(fr * (jnp.dot(a, w3_ref[...], precision=hi, preferred_element_type=F32)
                                   + b3_ref[...]))

    h = jnp.dot(a_ref[...], wo_ref[...], precision=hi, preferred_element_type=F32)
    h = h * jnp.exp(-t_ref[...] * ad_ref[...])
    backward = (j // 2) % 2 == 1
    row = lax.broadcasted_iota(jnp.int32, h.shape, 0)
    o_ref[...] = jnp.where(jnp.logical_and(backward, row == 0), 0.0, h)


def _pad2(a, rows, cols):
    return jnp.pad(a, ((0, rows - a.shape[0]), (0, cols - a.shape[1])))


def _hyena_filters(n, w1, b1, w2, b2, w3, b3, freq, w_out):
    t = jnp.linspace(0.0, 1.0, n, dtype=F32)[:, None]
    bands = (FILTER_EMB - 1) // 2
    w = 2.0 * math.pi * jnp.arange(n, dtype=F32)[:, None] / n
    f = jnp.linspace(1e-4, bands - 1, bands, dtype=F32)[None, :]
    z = jnp.concatenate([t, jnp.cos(f * w), -jnp.sin(f * w)], axis=-1)
    max_decay = math.log(DECAY_TARGET) / FAST_DECAY_PCT
    min_decay = math.log(DECAY_TARGET) / SLOW_DECAY_PCT
    ad = jnp.abs(jnp.linspace(min_decay, max_decay, HYENA_WIDTH, dtype=F32))[None, :]
    p = FILT_PAD
    tn = 1024
    n_out = w_out.shape[1]
    full = lambda shape: pl.BlockSpec(shape, lambda j: (0, 0))
    return pl.pallas_call(
        _filter_kernel,
        grid=(n_out // tn,),
        in_specs=[full((n, p)), full((p, p)), full((1, p)), full((p, p)), full((1, p)), full((p, p)),
                  full((1, p)), full((1, p)),
                  pl.BlockSpec((p, tn), lambda j: (0, j)),
                  full((n, 1)),
                  pl.BlockSpec((1, tn), lambda j: (0, j % (HYENA_WIDTH // tn)))],
        out_specs=pl.BlockSpec((n, tn), lambda j: (0, j)),
        out_shape=jax.ShapeDtypeStruct((n, n_out), F32),
        scratch_shapes=[pltpu.VMEM((n, p), F32)],
        compiler_params=_cparams(1),
        name="hyena_filters",
    )(_pad2(z, n, p), _pad2(w1, p, p), _pad2(b1[None, :], 1, p), _pad2(w2, p, p), _pad2(b2[None, :], 1, p),
      _pad2(w3, p, p), _pad2(b3[None, :], 1, p), _pad2(freq[None, :], 1, p), _pad2(w_out, p, n_out), t, ad)


def _dft_matrix_kernel(ca_ref, sa_ref, cb_ref, sb_ref, alt_ref, cs_ref, fre_ref, fim_ref, gre_ref, gim_ref):
    i = pl.program_id(0)
    ca, sa = ca_ref[...], sa_ref[...]
    cb, sb = cb_ref[...], sb_ref[...]
    cos = ca * cb - sa * sb
    sin = sa * cb + ca * sb
    cs = cs_ref[...]
    row = lax.broadcasted_iota(jnp.int32, cos.shape, 0)
    col = lax.broadcasted_iota(jnp.int32, cos.shape, 1)
    nyquist_row = jnp.logical_and(i == 0, row == 0)
    alt_rows = jnp.where(row % 2 == 0, 1.0, -1.0)
    fre_ref[...] = cos.astype(fre_ref.dtype)
    fim_ref[...] = jnp.where(nyquist_row, alt_ref[...], -sin).astype(fim_ref.dtype)
    gre_ref[...] = (cos * cs).astype(gre_ref.dtype)
    gim_ref[...] = (jnp.where(col == 0, alt_rows, -sin) * cs).astype(gim_ref.dtype)


def _dft_matrices(n):
    big = 2 * n
    rb = DFT_ROWS
    t = jnp.arange(n, dtype=jnp.int32)[None, :]
    unit = 2.0 * math.pi / big
    coarse = ((rb * jnp.arange(n // rb, dtype=jnp.int32)[:, None] * t) % big).astype(F32) * unit
    fine = ((jnp.arange(rb, dtype=jnp.int32)[:, None] * t) % big).astype(F32) * unit
    alt = jnp.where(t % 2 == 0, 1.0, -1.0).astype(F32)
    cs = jnp.where(t == 0, 1.0 / big, 2.0 / big).astype(F32)
    coarse_spec = pl.BlockSpec((None, 1, n), lambda i: (i, 0, 0))
    whole = lambda r: pl.BlockSpec((r, n), lambda i: (0, 0))
    out_spec = pl.BlockSpec((rb, n), lambda i: (i, 0))
    return pl.pallas_call(
        _dft_matrix_kernel,
        grid=(n // rb,),
        in_specs=[coarse_spec, coarse_spec, whole(rb), whole(rb), whole(1), whole(1)],
        out_specs=[out_spec] * 4,
        out_shape=[jax.ShapeDtypeStruct((n, n), BF16)] * 4,
        compiler_params=_cparams(1),
        name="dft_matrices",
    )(jnp.cos(coarse)[:, None, :], jnp.sin(coarse)[:, None, :], jnp.cos(fine), jnp.sin(fine), alt, cs)


def _spectrum_kernel(fre_ref, fim_ref, flo_ref, fhi_ref, blo_ref, bhi_ref, k0r_ref, k0i_ref, k1r_ref, k1i_ref,
                     kmr_ref, kmi_ref):
    fre = fre_ref[...]
    fim = fim_ref[...]

    def spectrum(x_ref):
        x = x_ref[...].astype(BF16)
        return jnp.dot(fre, x, preferred_element_type=F32), jnp.dot(fim, x, preferred_element_type=F32)

    alr, ali = spectrum(flo_ref)
    ahr, ahi = spectrum(fhi_ref)
    blr, bli = spectrum(blo_ref)
    bhr, bhi = spectrum(bhi_ref)
    h0 = flo_ref[0:1, :].astype(BF16).astype(F32)
    row = lax.broadcasted_iota(jnp.int32, alr.shape, 0)
    first = row == 0
    sgn = jnp.where(row % 2 == 0, 1.0, -1.0)
    k0r_ref[...] = alr + blr
    k0i_ref[...] = jnp.where(first, ali + bli, ali - bli)
    k1r_ref[...] = ahr + sgn * (alr - h0)
    k1i_ref[...] = jnp.where(first, ahi + ali - h0, ahi + sgn * ali)
    kmr_ref[...] = bhr + sgn * blr
    kmi_ref[...] = jnp.where(first, bhi + bli, -bhi - sgn * bli)


def _filter_spectra(h_time, f_re, f_im):
    blk = CONV_BLOCK
    tc = 256
    per = HYENA_WIDTH // tc
    whole = pl.BlockSpec((blk, blk), lambda j: (0, 0))
    taps = lambda half, direction: pl.BlockSpec(
        (blk, tc), lambda j: (half, (j // per) * 2 * per + direction * per + j % per))
    return pl.pallas_call(
        _spectrum_kernel,
        grid=(2 * per,),
        in_specs=[whole, whole, taps(0, 0), taps(1, 0), taps(0, 1), taps(1, 1)],
        out_specs=[pl.BlockSpec((blk, tc), lambda j: (0, j))] * 6,
        out_shape=[jax.ShapeDtypeStruct((blk, 2 * HYENA_WIDTH), F32)] * 6,
        compiler_params=_cparams(1),
        name="filter_spectra",
    )(f_re, f_im, h_time, h_time, h_time, h_time)


def _short_conv_kernel(p_ref, w_ref, b_ref, u_ref, v16_ref):
    j = pl.program_id(1)
    x = p_ref[...]
    n = x.shape[0]
    row = lax.broadcasted_iota(jnp.int32, x.shape, 0)
    prev = jnp.where(row == 0, 0.0, pltpu.roll(x, 1, axis=0))
    nxt = jnp.where(row == n - 1, 0.0, pltpu.roll(x, n - 1, axis=0))
    u = prev * w_ref[0:1, :] + x * w_ref[1:2, :] + nxt * w_ref[2:3, :] + b_ref[...]
    u_ref[...] = u

    @pl.when(j < HYENA_WIDTH // u.shape[1])
    def _():
        v16_ref[...] = u.astype(v16_ref.dtype)


def _short_conv(p3, w, b):
    bsz, n, _ = p3.shape
    tc = 512
    nj = 3 * HYENA_WIDTH // tc
    nv = HYENA_WIDTH // tc
    off = V_END // tc
    return pl.pallas_call(
        _short_conv_kernel,
        grid=(bsz, nj),
        in_specs=[pl.BlockSpec((None, n, tc), lambda bi, j: (bi, 0, off + j)),
                  pl.BlockSpec((3, tc), lambda bi, j: (0, j)),
                  pl.BlockSpec((1, tc), lambda bi, j: (0, j))],
        out_specs=[pl.BlockSpec((None, n, tc), lambda bi, j: (bi, 0, j)),
                   pl.BlockSpec((None, n, tc), lambda bi, j: (bi, 0, jnp.minimum(j, nv - 1)))],
        out_shape=[jax.ShapeDtypeStruct((bsz, n, 3 * HYENA_WIDTH), F32),
                   jax.ShapeDtypeStruct((bsz, n, HYENA_WIDTH), BF16)],
        compiler_params=_cparams(2),
        name="short_conv",
    )(p3, w, b)


def _packed_cmul(xr, xi, kr, ki, first):
    return (jnp.where(first, xr * kr, xr * kr - xi * ki), jnp.where(first, xi * ki, xr * ki + xi * kr))


def _dft_fwd_kernel(fre_ref, fim_ref, u_ref, k0r_ref, k0i_ref, k1r_ref, k1i_ref, kmr_ref, kmi_ref, yre_ref,
                    yim_ref):
    fre = fre_ref[...]
    fim = fim_ref[...]
    blk = fre.shape[0]
    u0 = u_ref[pl.ds(0, blk), :]
    u1 = u_ref[pl.ds(blk, blk), :]
    x0r = jnp.dot(fre, u0, preferred_element_type=F32)
    x0i = jnp.dot(fim, u0, preferred_element_type=F32)
    x1r = jnp.dot(fre, u1, preferred_element_type=F32)
    x1i = jnp.dot(fim, u1, preferred_element_type=F32)
    first = lax.broadcasted_iota(jnp.int32, x0r.shape, 0) == 0
    k0r, k0i = k0r_ref[...], k0i_ref[...]
    ar, ai = _packed_cmul(x0r, x0i, k0r, k0i, first)
    br, bi = _packed_cmul(x1r, x1i, kmr_ref[...], kmi_ref[...], first)
    yre_ref[0] = (ar + br).astype(yre_ref.dtype)
    yim_ref[0] = (ai + bi).astype(yim_ref.dtype)
    cr, ci = _packed_cmul(x0r, x0i, k1r_ref[...], k1i_ref[...], first)
    dr, di = _packed_cmul(x1r, x1i, k0r, k0i, first)
    yre_ref[1] = (cr + dr).astype(yre_ref.dtype)
    yim_ref[1] = (ci + di).astype(yim_ref.dtype)


def _dft_forward(u16, f_re, f_im, spectra, order):
    bsz, n, ch = u16.shape
    blk = CONV_BLOCK
    tc = 256
    nc = ch // tc
    whole = pl.BlockSpec((blk, blk), lambda bi, c: (0, 0))
    seg = pl.BlockSpec((blk, tc), lambda bi, c: (0, order * nc + c))
    return pl.pallas_call(
        _dft_fwd_kernel,
        grid=(bsz, nc),
        in_specs=[whole, whole, pl.BlockSpec((None, n, tc), lambda bi, c: (bi, 0, c))] + [seg] * 6,
        out_specs=[pl.BlockSpec((None, 2, blk, tc), lambda bi, c: (bi, 0, 0, c))] * 2,
        out_shape=[jax.ShapeDtypeStruct((bsz, 2, blk, ch), BF16)] * 2,
        compiler_params=_cparams(2),
        name="dft_forward",
    )(f_re, f_im, u16, *spectra)


def _dft_inv_kernel(gre_ref, gim_ref, yre_ref, yim_ref, u_ref, gate_ref, bias_ref, *o_refs):
    y = (jnp.dot(gre_ref[...], yre_ref[...], preferred_element_type=F32)
         + jnp.dot(gim_ref[...], yim_ref[...], preferred_element_type=F32))
    z = gate_ref[...] * (y + u_ref[...] * bias_ref[...])
    for o in o_refs:
        o[...] = z.astype(o.dtype)


def _dft_inverse(y_re, y_im, g_re, g_im, u_arr, u_blk0, gate_arr, gate_blk0, bias, out_dtypes):
    bsz, nb, blk, ch = y_re.shape
    tc = 512
    nc = ch // tc
    whole = pl.BlockSpec((blk, blk), lambda bi, c, t: (0, 0))
    spec = pl.BlockSpec((None, None, blk, tc), lambda bi, c, t: (bi, t, 0, c))
    return pl.pallas_call(
        _dft_inv_kernel,
        grid=(bsz, nc, nb),
        in_specs=[whole, whole, spec, spec,
                  pl.BlockSpec((None, blk, tc), lambda bi, c, t: (bi, t, u_blk0 * nc + c)),
                  pl.BlockSpec((None, blk, tc), lambda bi, c, t: (bi, t, gate_blk0 * nc + c)),
                  pl.BlockSpec((1, tc), lambda bi, c, t: (0, c))],
        out_specs=[pl.BlockSpec((None, blk, tc), lambda bi, c, t: (bi, t, c))] * len(out_dtypes),
        out_shape=[jax.ShapeDtypeStruct((bsz, nb * blk, ch), dt) for dt in out_dtypes],
        compiler_params=_cparams(3),
        name="dft_inverse",
    )(g_re, g_im, y_re, y_im, u_arr, gate_arr, bias)


def _ffn(h, w_gate, w_up, w_down, lead, rows, side=None, side_args=()):
    n_tiles = pl.cdiv(D_FF, MM_TN)
    outs = _matmul_wres([h], [w_gate, w_up], [(0, 0), (0, 1)], _ep_swiglu, D_FF, [BF16], tm=MM_TM, tn=MM_TN,
                        m_tiles=rows // MM_TM, n_tiles=n_tiles, k=D_MODEL, ch=MM_CHUNK, lead=[lead, lead],
                        last_width=D_FF - (n_tiles - 1) * MM_TN, side=side, side_args=side_args,
                        name="ffn_gate_up")
    f = _matmul_wres([outs[0]], [w_down], [(0, 0)], _ep_store, D_MODEL, [BRANCH_DTYPE], tm=DOWN_TM, tn=MM_TN,
                     m_tiles=rows // DOWN_TM, n_tiles=D_MODEL // MM_TN, k=D_FF, ch=DOWN_CHUNK, lead=[lead],
                     name="ffn_down")[0]
    return f, (outs[1] if side is not None else None)


def kernel(x, c, ctx, c_ctx, w_ada, b_ada, pre_g, post_g, ffn_w_gate, ffn_w_up, ffn_w_down, w_in, q_norm_g,
           k_norm_g, short_w, short_b, filt_w1, filt_b1, filt_w2, filt_b2, filt_w3, filt_b3, filt_freq,
           filt_w_out, hyena_bias, w_br_attn, w_br_hyena, w_out):
    bsz, n_lat, d = x.shape
    n_ctx = ctx.shape[1]
    rows_x, rows_c = bsz * n_lat, bsz * n_ctx
    x2d = x.reshape(rows_x, d)
    c2d = ctx.reshape(rows_c, d)
    l = 0

    cond = jnp.concatenate([c, c_ctx[None, :], jnp.zeros((3, d), F32)], axis=0)
    n_first = 2 * d
    bias = b_ada[l][None, :]
    mods_first = _ada(cond, w_ada, bias, l, n_first)
    pre = [pre_g[l, s][None, :] for s in range(3)]
    post = [post_g[l, s][None, :] for s in range(3)]

    h0 = _modulate_first(x2d, c2d, mods_first[:5].reshape(5, 2, d), pre[0], shift_i=0, scale_i=1)
    side = _SidePlan(lead=(l,), k=d, ts=ADA_SIDE_TILE, n_tiles=(N_MOD * d - n_first) // ADA_SIDE_TILE,
                     col0=n_first)
    f1, mods_rest = _ffn(h0, ffn_w_gate, ffn_w_up, ffn_w_down, (l, 0), rows_x + rows_c, side=side,
                         side_args=(cond, w_ada, bias[:, n_first:]))
    mods = jnp.concatenate([mods_first, mods_rest], axis=1)[:5].reshape(5, N_MOD, d)
    x1, h1 = _residual(f1, x2d, c2d, mods, post[0], pre[1], gate_i=2, weight=MACARON_W, shift_i=3, scale_i=4)

    tm, tn = MM_TM, MM_TN
    p = _matmul_wres([h1], [w_in], [(0, 0)], _ep_store, IN_COLS, [F32], tm=tm, tn=IN_TN, m_tiles=rows_x // tm,
                     n_tiles=IN_COLS // IN_TN, k=d, ch=IN_CHUNK, lead=[(l,)], name="in_proj")[0]
    pc_kv = _matmul_wres([h1], [w_in], [(0, 0)], _ep_store, 2 * KV_WIDTH, [F32], tm=CTX_TM, tn=tn,
                         m_tiles=rows_c // CTX_TM, n_tiles=2 * KV_WIDTH // tn, k=d, ch=MM_CHUNK, lead=[(l,)],
                         col0=Q_END, a_blk0=rows_x // CTX_TM, name="in_proj_ctx")[0]

    tables = _rope_tables(n_lat)
    q16, k16, v16 = _qkv_prep(p, q_norm_g[l][None, :], k_norm_g[l][None, :], tables)
    kc16, vc16 = _kv_prep_ctx(pc_kv, k_norm_g[l][None, :])
    k_all = jnp.concatenate([kc16.reshape(bsz, n_ctx, KV_WIDTH), k16.reshape(bsz, n_lat, KV_WIDTH)], axis=1)
    v_all = jnp.concatenate([vc16.reshape(bsz, n_ctx, KV_WIDTH), v16.reshape(bsz, n_lat, KV_WIDTH)], axis=1)
    attn_o = _attention(q16, k_all, v_all, n_lat)

    assert n_lat == 2 * CONV_BLOCK
    f_re, f_im, g_re, g_im = _dft_matrices(CONV_BLOCK)
    h_time = _hyena_filters(n_lat, filt_w1[l], filt_b1[l], filt_w2[l], filt_b2[l], filt_w3[l], filt_b3[l],
                            filt_freq[l], filt_w_out[l])
    spectra = _filter_spectra(h_time, f_re, f_im)
    p3 = p.reshape(bsz, n_lat, IN_COLS)
    u, v16h = _short_conv(p3, short_w[l], short_b[l][None, :])
    y_re, y_im = _dft_forward(v16h, f_re, f_im, spectra, order=0)
    z, z16 = _dft_inverse(y_re, y_im, g_re, g_im, u, 0, u, 1, hyena_bias[l, 0][None, :], [F32, BF16])
    y_re, y_im = _dft_forward(z16, f_re, f_im, spectra, order=1)
    hy_o = _dft_inverse(y_re, y_im, g_re, g_im, z, 0, u, 2, hyena_bias[l, 1][None, :], [BF16])[0]

    gate_spec = lambda off: pl.BlockSpec((tm, tn), lambda j, i: (i, off + j))
    merged = _matmul_wres([attn_o, hy_o.reshape(rows_x, HYENA_WIDTH)], [w_br_attn, w_br_hyena], [(0, 0), (1, 1)],
                          _ep_merge, d, [BF16], tm=tm, tn=tn, m_tiles=rows_x // tm, n_tiles=d // tn,
                          k=ATTN_WIDTH, ch=MM_CHUNK, lead=[(l,), (l,)], extras=(p, p),
                          extra_specs=(gate_spec(HY_END // tn), gate_spec((HY_END + d) // tn)), name="merge")[0]
    out = _matmul_wres([merged], [w_out], [(0, 0)], _ep_store, d, [BRANCH_DTYPE], tm=tm, tn=IN_TN,
                       m_tiles=rows_x // tm, n_tiles=d // IN_TN, k=d, ch=IN_CHUNK, lead=[(l,)], name="out_proj")[0]
    x2, h2 = _residual(out, x1, None, mods, post[1], pre[2], gate_i=5, weight=1.0, shift_i=6, scale_i=7)

    f2, _ = _ffn(h2, ffn_w_gate, ffn_w_up, ffn_w_down, (l, 1), rows_x)
    x3 = _residual(f2, x2, None, mods, post[2], None, gate_i=8, weight=MACARON_W)[0]
    return x3.reshape(bsz, n_lat, d)
```

```python
import functools
import math
from typing import NamedTuple

import jax
import jax.numpy as jnp
from jax import lax
from jax.experimental import pallas as pl
from jax.experimental.pallas import tpu as pltpu

F32 = jnp.float32
BF16 = jnp.bfloat16
BRANCH_DTYPE = BF16

D_MODEL = 4096
GRID_W = 64
HEAD_DIM = 128
N_Q_HEADS = 16
N_KV_HEADS = 4
Q_PER_KV = N_Q_HEADS // N_KV_HEADS
ATTN_WIDTH = N_Q_HEADS * HEAD_DIM
KV_WIDTH = N_KV_HEADS * HEAD_DIM
ROPE_THETA = 10000.0
AXIS_DIM = HEAD_DIM // 2
HYENA_WIDTH = D_MODEL // 2
FILTER_EMB = 33
FILTER_HIDDEN = 64
DECAY_TARGET = 1e-2
FAST_DECAY_PCT = 0.3
SLOW_DECAY_PCT = 1.5
D_FF = 11008
MACARON_W = 0.5
N_MOD = 9
NORM_EPS = 1e-6
Q_END = ATTN_WIDTH
K_END = Q_END + KV_WIDTH
V_END = K_END + KV_WIDTH
HY_END = V_END + 3 * HYENA_WIDTH
IN_COLS = HY_END + 2 * D_MODEL
SCORE_SCALE_LOG2 = math.log2(math.e) / math.sqrt(HEAD_DIM)

V7X_LANES = 128
BF16_SUBLANES = 16
V7X_VMEM_LIMIT_BYTES = 56 * 1024 * 1024

MM_TM = 1024
MM_TN = 512
MM_CHUNK = 1024
IN_TN = 1024
IN_CHUNK = 512
CTX_TM = 256
DOWN_TM = 512
DOWN_CHUNK = D_FF // 16
ROW_TILE = 256
ADA_SIDE_TILE = 256
CONV_BLOCK = 1024
DFT_ROWS = 64
FILT_PAD = V7X_LANES


def _cparams(n_axes):
    return pltpu.CompilerParams(dimension_semantics=("arbitrary",) * n_axes,
                                vmem_limit_bytes=V7X_VMEM_LIMIT_BYTES)


def _rms(x):
    return x * lax.rsqrt(jnp.mean(x * x, axis=-1, keepdims=True) + NORM_EPS)


def _ada_kernel(c_ref, w_ref, b_ref, o_ref):
    c = c_ref[...]
    s = (c * jax.nn.sigmoid(c)).astype(BF16)
    o_ref[...] = jnp.dot(s, w_ref[...].astype(BF16), preferred_element_type=F32) + b_ref[...]


def _ada(cond, w, b, layer, n):
    tn = 512
    return pl.pallas_call(
        _ada_kernel,
        grid=(n // tn,),
        in_specs=[pl.BlockSpec((8, D_MODEL), lambda j: (0, 0)),
                  pl.BlockSpec((None, D_MODEL, tn), lambda j: (layer, 0, j)),
                  pl.BlockSpec((1, tn), lambda j: (0, j))],
        out_specs=pl.BlockSpec((8, tn), lambda j: (0, j)),
        out_shape=jax.ShapeDtypeStruct((8, n), F32),
        compiler_params=_cparams(1),
        name="ada",
    )(cond, w, b)


def _mod_index(i, rows_per_batch_tiles):
    return jnp.minimum(i // rows_per_batch_tiles, 4)


def _modulate_kernel(x_ref, c_ref, m_ref, g_ref, o_ref, *, nx, shift_i, scale_i):
    i = pl.program_id(0)

    def body(src):
        y = _rms(src[...]) * g_ref[...]
        o_ref[...] = (y * (1.0 + m_ref[0, scale_i:scale_i + 1, :])
                      + m_ref[0, shift_i:shift_i + 1, :]).astype(o_ref.dtype)

    pl.when(i < nx)(lambda: body(x_ref))
    pl.when(i >= nx)(lambda: body(c_ref))


def _modulate_first(x2d, c2d, mods, g, shift_i, scale_i):
    tr = ROW_TILE
    nx, nc = x2d.shape[0] // tr, c2d.shape[0] // tr
    per_batch = (x2d.shape[0] // 4) // tr
    return pl.pallas_call(
        functools.partial(_modulate_kernel, nx=nx, shift_i=shift_i, scale_i=scale_i),
        grid=(nx + nc,),
        in_specs=[pl.BlockSpec((tr, D_MODEL), lambda i: (jnp.minimum(i, nx - 1), 0)),
                  pl.BlockSpec((tr, D_MODEL), lambda i: (jnp.maximum(i - nx, 0), 0)),
                  pl.BlockSpec((1, mods.shape[1], D_MODEL), lambda i: (_mod_index(i, per_batch), 0, 0)),
                  pl.BlockSpec((1, D_MODEL), lambda i: (0, 0))],
        out_specs=pl.BlockSpec((tr, D_MODEL), lambda i: (i, 0)),
        out_shape=jax.ShapeDtypeStruct(((nx + nc) * tr, D_MODEL), BF16),
        compiler_params=_cparams(1),
        name="modulate_first",
    )(x2d, c2d, mods, g)


def _resid_kernel(*refs, nx, two_source, gate_i, weight, shift_i, scale_i, with_next):
    it = iter(refs)
    f_ref = next(it)
    x_ref = next(it)
    c_ref = next(it) if two_source else None
    m_ref = next(it)
    pg_ref = next(it)
    ng_ref = next(it) if with_next else None
    xo_ref = next(it)
    ho_ref = next(it) if with_next else None
    i = pl.program_id(0)

    def body(src):
        y = _rms(f_ref[...].astype(F32)) * pg_ref[...]
        xn = src[...] + (weight * m_ref[0, gate_i:gate_i + 1, :]) * y
        xo_ref[...] = xn
        if with_next:
            h = _rms(xn) * ng_ref[...]
            ho_ref[...] = (h * (1.0 + m_ref[0, scale_i:scale_i + 1, :])
                           + m_ref[0, shift_i:shift_i + 1, :]).astype(ho_ref.dtype)

    if two_source:
        pl.when(i < nx)(lambda: body(x_ref))
        pl.when(i >= nx)(lambda: body(c_ref))
    else:
        body(x_ref)


def _residual(f, x2d, c2d, mods, post_g, next_g, *, gate_i, weight, shift_i=0, scale_i=0):
    tr = ROW_TILE
    rows = f.shape[0]
    n = rows // tr
    two_source = c2d is not None
    with_next = next_g is not None
    nx = x2d.shape[0] // tr if two_source else n
    per_batch = (8192 // 4) // tr
    row = pl.BlockSpec((tr, D_MODEL), lambda i: (i, 0))
    vec = pl.BlockSpec((1, D_MODEL), lambda i: (0, 0))
    in_specs = [row]
    args = [f]
    if two_source:
        in_specs += [pl.BlockSpec((tr, D_MODEL), lambda i: (jnp.minimum(i, nx - 1), 0)),
                     pl.BlockSpec((tr, D_MODEL), lambda i: (jnp.maximum(i - nx, 0), 0))]
        args += [x2d, c2d]
    else:
        in_specs += [row]
        args += [x2d]
    in_specs += [pl.BlockSpec((1, N_MOD, D_MODEL), lambda i: (_mod_index(i, per_batch), 0, 0)), vec]
    args += [mods, post_g]
    out_specs = [row]
    out_shape = [jax.ShapeDtypeStruct((rows, D_MODEL), F32)]
    if with_next:
        in_specs += [vec]
        args += [next_g]
        out_specs += [row]
        out_shape += [jax.ShapeDtypeStruct((rows, D_MODEL), BF16)]
    return pl.pallas_call(
        functools.partial(_resid_kernel, nx=nx, two_source=two_source, gate_i=gate_i, weight=weight,
                          shift_i=shift_i, scale_i=scale_i, with_next=with_next),
        grid=(n,),
        in_specs=in_specs,
        out_specs=out_specs,
        out_shape=out_shape,
        compiler_params=_cparams(1),
        name="residual",
    )(*args)


class _WeightPlan(NamedTuple):
    k: int
    ch: int
    tn: int
    n_tiles: int
    m_tiles: int
    col0: int
    last_width: int
    lead: tuple


def _wres_kernel(*refs, n_a, n_w, pairs, n_extra, n_out, epilogue, plan, side):
    it = iter(refs)
    a_refs = [next(it) for _ in range(n_a)]
    w_refs = [next(it) for _ in range(n_w)]
    e_refs = [next(it) for _ in range(n_extra)]
    if side is not None:
        side_c_ref, side_w_ref, side_b_ref = next(it), next(it), next(it)
    o_refs = [next(it) for _ in range(n_out)]
    if side is not None:
        side_o_ref = next(it)
    wbf, stage, sem = next(it), next(it), next(it)
    if side is not None:
        side_stage, side_sem = next(it), next(it)
    j = pl.program_id(0)
    i = pl.program_id(1)
    slot = j % 2
    per_w = plan.k // plan.ch
    n_chunks = n_w * per_w
    ragged = plan.last_width != plan.tn

    def chunk_copy(tile, c, width):
        w, r = divmod(c, per_w)
        col = plan.col0 + tile * plan.tn
        if not isinstance(col, int):
            col = pl.multiple_of(col, V7X_LANES)
        src = w_refs[w].at[(*plan.lead[w], pl.ds(r * plan.ch, plan.ch), pl.ds(col, width))]
        return pltpu.make_async_copy(src, stage.at[c % 2, :, pl.ds(0, width)], sem.at[c % 2])

    def width_variants(tile):
        if not ragged:
            return [(plan.tn, tile < plan.n_tiles)]
        return [(plan.tn, tile < plan.n_tiles - 1), (plan.last_width, tile == plan.n_tiles - 1)]

    @pl.when(jnp.logical_and(j == 0, i == 0))
    def _():
        chunk_copy(0, 0, plan.tn).start()
        for c in range(n_chunks):
            w, r = divmod(c, per_w)
            chunk_copy(0, c, plan.tn).wait()
            if c + 1 < n_chunks:
                chunk_copy(0, c + 1, plan.tn).start()
            wbf[0, w, pl.ds(r * plan.ch, plan.ch), :] = stage[c % 2].astype(BF16)
        second = plan.last_width if plan.n_tiles == 2 else plan.tn
        chunk_copy(1, 0, second).start()

    for width, cond in width_variants(j + 1):
        for c in range(n_chunks):
            @pl.when(jnp.logical_and(cond, i == c))
            def _(width=width, c=c):
                chunk_copy(j + 1, c, width).wait()
                if c + 1 < n_chunks:
                    chunk_copy(j + 1, c + 1, width).start()

    for width, cond in width_variants(j + 2):
        @pl.when(jnp.logical_and(cond, i == plan.m_tiles - 1))
        def _(width=width):
            chunk_copy(j + 2, 0, width).start()

    if side is not None:
        _side_stream_dma(side, side_w_ref, side_stage, side_sem, j * plan.m_tiles + i)

    def compute(width):
        c_now = jnp.minimum(i, n_chunks - 1)
        w_now = c_now // per_w
        row_now = pl.multiple_of((c_now % per_w) * plan.ch, BF16_SUBLANES)
        wbf[1 - slot, w_now, pl.ds(row_now, plan.ch), :] = stage[c_now % 2].astype(BF16)
        if side is not None:
            _side_stream_compute(side, side_c_ref, side_b_ref, side_o_ref, side_stage, j * plan.m_tiles + i)
        prods = [jnp.dot(a_refs[ai][...], wbf[slot, wi, :, pl.ds(0, width)], preferred_element_type=F32)
                 for ai, wi in pairs]
        epilogue(prods, e_refs, o_refs, width)

    if ragged:
        pl.when(j < plan.n_tiles - 1)(lambda: compute(plan.tn))
        pl.when(j == plan.n_tiles - 1)(lambda: compute(plan.last_width))
    else:
        compute(plan.tn)


class _SidePlan(NamedTuple):
    lead: tuple
    k: int
    ts: int
    n_tiles: int
    col0: int


def _side_stream_dma(side, w_ref, stage, sem, g):
    def copy(tile, slot):
        col = pl.multiple_of(side.col0 + tile * side.ts, V7X_LANES)
        return pltpu.make_async_copy(w_ref.at[(*side.lead, slice(None), pl.ds(col, side.ts))],
                                     stage.at[slot], sem.at[slot])

    pl.when(g == 0)(lambda: copy(g, 0).start())
    for slot in range(2):
        pl.when(jnp.logical_and(g < side.n_tiles, g % 2 == slot))(lambda slot=slot: copy(g, slot).wait())
    for slot in range(2):
        pl.when(jnp.logical_and(g + 1 < side.n_tiles, (g + 1) % 2 == slot))(
            lambda slot=slot: copy(g + 1, slot).start())


def _side_stream_compute(side, c_ref, b_ref, o_ref, stage, g):
    t = jnp.minimum(g, side.n_tiles - 1)
    c = c_ref[...]
    s = (c * jax.nn.sigmoid(c)).astype(BF16)
    o_ref[...] = jnp.dot(s, stage[t % 2].astype(BF16), preferred_element_type=F32) + b_ref[...]


def _matmul_wres(a_list, w_list, pairs, epilogue, out_cols, out_dtypes, *, tm, tn, m_tiles, n_tiles, k, ch,
                 lead, col0=0, last_width=None, a_blk0=0, extras=(), extra_specs=(), side=None, side_args=(),
                 name="matmul"):
    plan = _WeightPlan(k=k, ch=ch, tn=tn, n_tiles=n_tiles, m_tiles=m_tiles, col0=col0,
                       last_width=last_width or tn, lead=tuple(lead))
    n_chunks = len(w_list) * (k // ch)
    assert n_chunks <= m_tiles and n_chunks % 2 == 0 and n_tiles >= 2 and k % ch == 0
    in_specs = [pl.BlockSpec((tm, k), lambda j, i: (i + a_blk0, 0)) for _ in a_list]
    in_specs += [pl.BlockSpec(memory_space=pl.ANY) for _ in w_list]
    in_specs += list(extra_specs)
    out_specs = [pl.BlockSpec((tm, tn), lambda j, i: (i, j)) for _ in out_dtypes]
    out_shape = [jax.ShapeDtypeStruct((m_tiles * tm, out_cols), dt) for dt in out_dtypes]
    scratch = [pltpu.VMEM((2, len(w_list), k, tn), BF16), pltpu.VMEM((2, ch, tn), F32),
               pltpu.SemaphoreType.DMA((2,))]
    if side is not None:
        assert side.n_tiles <= n_tiles * m_tiles
        side_tile = lambda j, i: (0, jnp.minimum(j * m_tiles + i, side.n_tiles - 1))
        in_specs += [pl.BlockSpec((8, side.k), lambda j, i: (0, 0)), pl.BlockSpec(memory_space=pl.ANY),
                     pl.BlockSpec((1, side.ts), side_tile)]
        out_specs += [pl.BlockSpec((8, side.ts), side_tile)]
        out_shape += [jax.ShapeDtypeStruct((8, side.n_tiles * side.ts), F32)]
        scratch += [pltpu.VMEM((2, side.k, side.ts), F32), pltpu.SemaphoreType.DMA((2,))]
    return pl.pallas_call(
        functools.partial(_wres_kernel, n_a=len(a_list), n_w=len(w_list), pairs=tuple(pairs),
                          n_extra=len(extras), n_out=len(out_dtypes), epilogue=epilogue, plan=plan, side=side),
        grid=(n_tiles, m_tiles),
        in_specs=in_specs,
        out_specs=out_specs,
        out_shape=out_shape,
        scratch_shapes=scratch,
        compiler_params=_cparams(2),
        name=name,
    )(*a_list, *w_list, *extras, *side_args)


def _ep_store(accs, e_refs, o_refs, width):
    o_refs[0][:, pl.ds(0, width)] = accs[0].astype(o_refs[0].dtype)


def _ep_swiglu(accs, e_refs, o_refs, width):
    g, u = accs
    o_refs[0][:, pl.ds(0, width)] = (g * jax.nn.sigmoid(g) * u).astype(o_refs[0].dtype)


def _ep_merge(accs, e_refs, o_refs, width):
    a, h = accs
    ga, gh = e_refs
    o_refs[0][...] = (jax.nn.sigmoid(ga[...]) * a + jax.nn.sigmoid(gh[...]) * h).astype(o_refs[0].dtype)


def _rope_tables(n_lat):
    t = jnp.arange(n_lat, dtype=jnp.int32)
    row = (t // GRID_W).astype(F32)
    col = (t % GRID_W).astype(F32)
    inv_freq = ROPE_THETA ** (-jnp.arange(0, AXIS_DIM, 2, dtype=F32) / AXIS_DIM)
    ang = jnp.concatenate([jnp.tile(row[:, None] * inv_freq, (1, 2)),
                           jnp.tile(col[:, None] * inv_freq, (1, 2))], axis=-1)
    lane = jnp.arange(HEAD_DIM, dtype=jnp.int32)
    first = (lane % AXIS_DIM) < (AXIS_DIM // 2)
    cos = jnp.cos(ang)
    sin = jnp.sin(ang)
    sin_a = jnp.where(first, -sin, 0.0)
    sin_b = jnp.where(first, 0.0, sin)
    return cos, sin_a, sin_b


def _head_prep(x, g, cos, sin_a, sin_b, rope):
    y = _rms(x) * g
    if rope:
        up = pltpu.roll(y, HEAD_DIM - AXIS_DIM // 2, axis=1)
        dn = pltpu.roll(y, AXIS_DIM // 2, axis=1)
        y = y * cos + up * sin_a + dn * sin_b
    return y


def _qkv_prep_kernel(q_ref, kv_ref, qg_ref, kg_ref, cos_ref, sa_ref, sb_ref, qo_ref, ko_ref, vo_ref):
    cos, sa, sb = cos_ref[...], sa_ref[...], sb_ref[...]
    for h in range(N_Q_HEADS):
        sl = slice(h * HEAD_DIM, (h + 1) * HEAD_DIM)
        q = _head_prep(q_ref[:, sl], qg_ref[...], cos, sa, sb, True)
        qo_ref[:, sl] = (q * SCORE_SCALE_LOG2).astype(qo_ref.dtype)
    for h in range(N_KV_HEADS):
        sl = slice(h * HEAD_DIM, (h + 1) * HEAD_DIM)
        ko_ref[:, sl] = _head_prep(kv_ref[:, sl], kg_ref[...], cos, sa, sb, True).astype(ko_ref.dtype)
    vo_ref[...] = kv_ref[:, KV_WIDTH:].astype(vo_ref.dtype)


def _qkv_prep(p, q_g, k_g, tables):
    tr = ROW_TILE
    rows = p.shape[0]
    cos, sin_a, sin_b = tables
    nt = cos.shape[0] // tr
    tab = pl.BlockSpec((tr, HEAD_DIM), lambda i: (i % nt, 0))
    vec = pl.BlockSpec((1, HEAD_DIM), lambda i: (0, 0))
    kv_out = pl.BlockSpec((tr, KV_WIDTH), lambda i: (i, 0))
    return pl.pallas_call(
        _qkv_prep_kernel,
        grid=(rows // tr,),
        in_specs=[pl.BlockSpec((tr, ATTN_WIDTH), lambda i: (i, 0)),
                  pl.BlockSpec((tr, 2 * KV_WIDTH), lambda i: (i, Q_END // (2 * KV_WIDTH))),
                  vec, vec, tab, tab, tab],
        out_specs=[pl.BlockSpec((tr, ATTN_WIDTH), lambda i: (i, 0)), kv_out, kv_out],
        out_shape=[jax.ShapeDtypeStruct((rows, ATTN_WIDTH), BF16),
                   jax.ShapeDtypeStruct((rows, KV_WIDTH), BF16),
                   jax.ShapeDtypeStruct((rows, KV_WIDTH), BF16)],
        compiler_params=_cparams(1),
        name="qkv_prep",
    )(p, p, q_g, k_g, cos, sin_a, sin_b)


def _kv_prep_ctx_kernel(kv_ref, kg_ref, ko_ref, vo_ref):
    for h in range(N_KV_HEADS):
        sl = slice(h * HEAD_DIM, (h + 1) * HEAD_DIM)
        ko_ref[:, sl] = _head_prep(kv_ref[:, sl], kg_ref[...], None, None, None, False).astype(ko_ref.dtype)
    vo_ref[...] = kv_ref[:, KV_WIDTH:].astype(vo_ref.dtype)


def _kv_prep_ctx(pc_kv, k_g):
    tr = ROW_TILE
    rows = pc_kv.shape[0]
    kv_out = pl.BlockSpec((tr, KV_WIDTH), lambda i: (i, 0))
    return pl.pallas_call(
        _kv_prep_ctx_kernel,
        grid=(rows // tr,),
        in_specs=[pl.BlockSpec((tr, 2 * KV_WIDTH), lambda i: (i, 0)),
                  pl.BlockSpec((1, HEAD_DIM), lambda i: (0, 0))],
        out_specs=[kv_out, kv_out],
        out_shape=[jax.ShapeDtypeStruct((rows, KV_WIDTH), BF16)] * 2,
        compiler_params=_cparams(1),
        name="kv_prep_ctx",
    )(pc_kv, k_g)


def _attn_kernel(q_ref, k_ref, v_ref, o_ref):
    k = k_ref[...]
    v = v_ref[...]
    for g in range(Q_PER_KV):
        sl = slice(g * HEAD_DIM, (g + 1) * HEAD_DIM)
        s = lax.dot_general(q_ref[:, sl], k, (((1,), (1,)), ((), ())), preferred_element_type=F32)
        m = jnp.max(s, axis=-1, keepdims=True)
        p = jnp.exp2(s - m)
        l = jnp.sum(p, axis=-1, keepdims=True)
        o = jnp.dot(p.astype(BF16), v, preferred_element_type=F32)
        o_ref[:, sl] = (o / l).astype(o_ref.dtype)


def _attention(q, k_all, v_all, n_lat):
    tq = 512
    b, t, _ = k_all.shape
    nq = n_lat // tq
    gw = Q_PER_KV * HEAD_DIM
    return pl.pallas_call(
        _attn_kernel,
        grid=(b, N_KV_HEADS, nq),
        in_specs=[pl.BlockSpec((tq, gw), lambda bi, h, i: (bi * nq + i, h)),
                  pl.BlockSpec((None, t, HEAD_DIM), lambda bi, h, i: (bi, 0, h)),
                  pl.BlockSpec((None, t, HEAD_DIM), lambda bi, h, i: (bi, 0, h))],
        out_specs=pl.BlockSpec((tq, gw), lambda bi, h, i: (bi * nq + i, h)),
        out_shape=jax.ShapeDtypeStruct(q.shape, BF16),
        compiler_params=_cparams(3),
        name="attention",
    )(q, k_all, v_all)


def _filter_kernel(z_ref, w1_ref, b1_ref, w2_ref, b2_ref, w3_ref, b3_ref, fr_ref, wo_ref, t_ref, ad_ref, o_ref,
                   a_ref):
    j = pl.program_id(0)
    hi = lax.Precision.HIGHEST

    @pl.when(j == 0)
    def _():
        fr = fr_ref[...]
        a = jnp.sin(fr * (jnp.dot(z_ref[...], w1_ref[...], precision=hi, preferred_element_type=F32)
                          + b1_ref[...]))
        a = jnp.sin(fr * (jnp.dot(a, w2_ref[...], precision=hi, preferred_element_type=F32) + b2_ref[...]))
        a_ref[...] = jnp.sin(fr * (jnp.dot(a, w3_ref[...], precision=hi, preferred_element_type=F32)
                                   + b3_ref[...]))

    h = jnp.dot(a_ref[...].astype(BF16), wo_ref[...].astype(BF16), preferred_element_type=F32)
    h = h * jnp.exp(-t_ref[...] * ad_ref[...])
    backward = (j // 2) % 2 == 1
    row = lax.broadcasted_iota(jnp.int32, h.shape, 0)
    o_ref[...] = jnp.where(jnp.logical_and(backward, row == 0), 0.0, h)


def _pad2(a, rows, cols):
    return jnp.pad(a, ((0, rows - a.shape[0]), (0, cols - a.shape[1])))


def _hyena_filters(n, w1, b1, w2, b2, w3, b3, freq, w_out):
    t = jnp.linspace(0.0, 1.0, n, dtype=F32)[:, None]
    bands = (FILTER_EMB - 1) // 2
    w = 2.0 * math.pi * jnp.arange(n, dtype=F32)[:, None] / n
    f = jnp.linspace(1e-4, bands - 1, bands, dtype=F32)[None, :]
    z = jnp.concatenate([t, jnp.cos(f * w), -jnp.sin(f * w)], axis=-1)
    max_decay = math.log(DECAY_TARGET) / FAST_DECAY_PCT
    min_decay = math.log(DECAY_TARGET) / SLOW_DECAY_PCT
    ad = jnp.abs(jnp.linspace(min_decay, max_decay, HYENA_WIDTH, dtype=F32))[None, :]
    p = FILT_PAD
    tn = 1024
    n_out = w_out.shape[1]
    full = lambda shape: pl.BlockSpec(shape, lambda j: (0, 0))
    return pl.pallas_call(
        _filter_kernel,
        grid=(n_out // tn,),
        in_specs=[full((n, p)), full((p, p)), full((1, p)), full((p, p)), full((1, p)), full((p, p)),
                  full((1, p)), full((1, p)),
                  pl.BlockSpec((p, tn), lambda j: (0, j)),
                  full((n, 1)),
                  pl.BlockSpec((1, tn), lambda j: (0, j % (HYENA_WIDTH // tn)))],
        out_specs=pl.BlockSpec((n, tn), lambda j: (0, j)),
        out_shape=jax.ShapeDtypeStruct((n, n_out), F32),
        scratch_shapes=[pltpu.VMEM((n, p), F32)],
        compiler_params=_cparams(1),
        name="hyena_filters",
    )(_pad2(z, n, p), _pad2(w1, p, p), _pad2(b1[None, :], 1, p), _pad2(w2, p, p), _pad2(b2[None, :], 1, p),
      _pad2(w3, p, p), _pad2(b3[None, :], 1, p), _pad2(freq[None, :], 1, p), _pad2(w_out, p, n_out), t, ad)


def _dft_matrix_kernel(ca_ref, sa_ref, cb_ref, sb_ref, alt_ref, cs_ref, fre_ref, fim_ref, gre_ref, gim_ref):
    i = pl.program_id(0)
    ca, sa = ca_ref[...], sa_ref[...]
    cb, sb = cb_ref[...], sb_ref[...]
    cos = ca * cb - sa * sb
    sin = sa * cb + ca * sb
    cs = cs_ref[...]
    row = lax.broadcasted_iota(jnp.int32, cos.shape, 0)
    col = lax.broadcasted_iota(jnp.int32, cos.shape, 1)
    nyquist_row = jnp.logical_and(i == 0, row == 0)
    alt_rows = jnp.where(row % 2 == 0, 1.0, -1.0)
    fre_ref[...] = cos.astype(fre_ref.dtype)
    fim_ref[...] = jnp.where(nyquist_row, alt_ref[...], -sin).astype(fim_ref.dtype)
    gre_ref[...] = (cos * cs).astype(gre_ref.dtype)
    gim_ref[...] = (jnp.where(col == 0, alt_rows, -sin) * cs).astype(gim_ref.dtype)


def _dft_matrices(n):
    big = 2 * n
    rb = DFT_ROWS
    t = jnp.arange(n, dtype=jnp.int32)[None, :]
    unit = 2.0 * math.pi / big
    coarse = ((rb * jnp.arange(n // rb, dtype=jnp.int32)[:, None] * t) % big).astype(F32) * unit
    fine = ((jnp.arange(rb, dtype=jnp.int32)[:, None] * t) % big).astype(F32) * unit
    alt = jnp.where(t % 2 == 0, 1.0, -1.0).astype(F32)
    cs = jnp.where(t == 0, 1.0 / big, 2.0 / big).astype(F32)
    coarse_spec = pl.BlockSpec((None, 1, n), lambda i: (i, 0, 0))
    whole = lambda r: pl.BlockSpec((r, n), lambda i: (0, 0))
    out_spec = pl.BlockSpec((rb, n), lambda i: (i, 0))
    return pl.pallas_call(
        _dft_matrix_kernel,
        grid=(n // rb,),
        in_specs=[coarse_spec, coarse_spec, whole(rb), whole(rb), whole(1), whole(1)],
        out_specs=[out_spec] * 4,
        out_shape=[jax.ShapeDtypeStruct((n, n), BF16)] * 4,
        compiler_params=_cparams(1),
        name="dft_matrices",
    )(jnp.cos(coarse)[:, None, :], jnp.sin(coarse)[:, None, :], jnp.cos(fine), jnp.sin(fine), alt, cs)


def _spectrum_kernel(fre_ref, fim_ref, flo_ref, fhi_ref, blo_ref, bhi_ref, k0r_ref, k0i_ref, k1r_ref, k1i_ref,
                     kmr_ref, kmi_ref):
    fre = fre_ref[...]
    fim = fim_ref[...]

    def spectrum(x_ref):
        x = x_ref[...].astype(BF16)
        return jnp.dot(fre, x, preferred_element_type=F32), jnp.dot(fim, x, preferred_element_type=F32)

    alr, ali = spectrum(flo_ref)
    ahr, ahi = spectrum(fhi_ref)
    blr, bli = spectrum(blo_ref)
    bhr, bhi = spectrum(bhi_ref)
    h0 = flo_ref[0:1, :].astype(BF16).astype(F32)
    row = lax.broadcasted_iota(jnp.int32, alr.shape, 0)
    first = row == 0
    sgn = jnp.where(row % 2 == 0, 1.0, -1.0)
    k0r_ref[...] = alr + blr
    k0i_ref[...] = jnp.where(first, ali + bli, ali - bli)
    k1r_ref[...] = ahr + sgn * (alr - h0)
    k1i_ref[...] = jnp.where(first, ahi + ali - h0, ahi + sgn * ali)
    kmr_ref[...] = bhr + sgn * blr
    kmi_ref[...] = jnp.where(first, bhi + bli, -bhi - sgn * bli)


def _filter_spectra(h_time, f_re, f_im):
    blk = CONV_BLOCK
    tc = 256
    per = HYENA_WIDTH // tc
    whole = pl.BlockSpec((blk, blk), lambda j: (0, 0))
    taps = lambda half, direction: pl.BlockSpec(
        (blk, tc), lambda j: (half, (j // per) * 2 * per + direction * per + j % per))
    return pl.pallas_call(
        _spectrum_kernel,
        grid=(2 * per,),
        in_specs=[whole, whole, taps(0, 0), taps(1, 0), taps(0, 1), taps(1, 1)],
        out_specs=[pl.BlockSpec((blk, tc), lambda j: (0, j))] * 6,
        out_shape=[jax.ShapeDtypeStruct((blk, 2 * HYENA_WIDTH), F32)] * 6,
        compiler_params=_cparams(1),
        name="filter_spectra",
    )(f_re, f_im, h_time, h_time, h_time, h_time)


def _short_conv_kernel(p_ref, w_ref, b_ref, u_ref, v16_ref):
    j = pl.program_id(1)
    x = p_ref[...]
    n = x.shape[0]
    row = lax.broadcasted_iota(jnp.int32, x.shape, 0)
    prev = jnp.where(row == 0, 0.0, pltpu.roll(x, 1, axis=0))
    nxt = jnp.where(row == n - 1, 0.0, pltpu.roll(x, n - 1, axis=0))
    u = prev * w_ref[0:1, :] + x * w_ref[1:2, :] + nxt * w_ref[2:3, :] + b_ref[...]
    u_ref[...] = u

    @pl.when(j < HYENA_WIDTH // u.shape[1])
    def _():
        v16_ref[...] = u.astype(v16_ref.dtype)


def _short_conv(p3, w, b):
    bsz, n, _ = p3.shape
    tc = 512
    nj = 3 * HYENA_WIDTH // tc
    nv = HYENA_WIDTH // tc
    off = V_END // tc
    return pl.pallas_call(
        _short_conv_kernel,
        grid=(bsz, nj),
        in_specs=[pl.BlockSpec((None, n, tc), lambda bi, j: (bi, 0, off + j)),
                  pl.BlockSpec((3, tc), lambda bi, j: (0, j)),
                  pl.BlockSpec((1, tc), lambda bi, j: (0, j))],
        out_specs=[pl.BlockSpec((None, n, tc), lambda bi, j: (bi, 0, j)),
                   pl.BlockSpec((None, n, tc), lambda bi, j: (bi, 0, jnp.minimum(j, nv - 1)))],
        out_shape=[jax.ShapeDtypeStruct((bsz, n, 3 * HYENA_WIDTH), F32),
                   jax.ShapeDtypeStruct((bsz, n, HYENA_WIDTH), BF16)],
        compiler_params=_cparams(2),
        name="short_conv",
    )(p3, w, b)


def _packed_cmul(xr, xi, kr, ki, first):
    return (jnp.where(first, xr * kr, xr * kr - xi * ki), jnp.where(first, xi * ki, xr * ki + xi * kr))


def _dft_fwd_kernel(fre_ref, fim_ref, u_ref, k0r_ref, k0i_ref, k1r_ref, k1i_ref, kmr_ref, kmi_ref, yre_ref,
                    yim_ref):
    fre = fre_ref[...]
    fim = fim_ref[...]
    blk = fre.shape[0]
    u0 = u_ref[pl.ds(0, blk), :]
    u1 = u_ref[pl.ds(blk, blk), :]
    x0r = jnp.dot(fre, u0, preferred_element_type=F32)
    x0i = jnp.dot(fim, u0, preferred_element_type=F32)
    x1r = jnp.dot(fre, u1, preferred_element_type=F32)
    x1i = jnp.dot(fim, u1, preferred_element_type=F32)
    first = lax.broadcasted_iota(jnp.int32, x0r.shape, 0) == 0
    k0r, k0i = k0r_ref[...], k0i_ref[...]
    ar, ai = _packed_cmul(x0r, x0i, k0r, k0i, first)
    br, bi = _packed_cmul(x1r, x1i, kmr_ref[...], kmi_ref[...], first)
    yre_ref[0] = (ar + br).astype(yre_ref.dtype)
    yim_ref[0] = (ai + bi).astype(yim_ref.dtype)
    cr, ci = _packed_cmul(x0r, x0i, k1r_ref[...], k1i_ref[...], first)
    dr, di = _packed_cmul(x1r, x1i, k0r, k0i, first)
    yre_ref[1] = (cr + dr).astype(yre_ref.dtype)
    yim_ref[1] = (ci + di).astype(yim_ref.dtype)


def _dft_forward(u16, f_re, f_im, spectra, order):
    bsz, n, ch = u16.shape
    blk = CONV_BLOCK
    tc = 256
    nc = ch // tc
    whole = pl.BlockSpec((blk, blk), lambda bi, c: (0, 0))
    seg = pl.BlockSpec((blk, tc), lambda bi, c: (0, order * nc + c))
    return pl.pallas_call(
        _dft_fwd_kernel,
        grid=(bsz, nc),
        in_specs=[whole, whole, pl.BlockSpec((None, n, tc), lambda bi, c: (bi, 0, c))] + [seg] * 6,
        out_specs=[pl.BlockSpec((None, 2, blk, tc), lambda bi, c: (bi, 0, 0, c))] * 2,
        out_shape=[jax.ShapeDtypeStruct((bsz, 2, blk, ch), BF16)] * 2,
        compiler_params=_cparams(2),
        name="dft_forward",
    )(f_re, f_im, u16, *spectra)


def _dft_inv_kernel(gre_ref, gim_ref, yre_ref, yim_ref, u_ref, gate_ref, bias_ref, *o_refs):
    y = (jnp.dot(gre_ref[...], yre_ref[...], preferred_element_type=F32)
         + jnp.dot(gim_ref[...], yim_ref[...], preferred_element_type=F32))
    z = gate_ref[...] * (y + u_ref[...] * bias_ref[...])
    for o in o_refs:
        o[...] = z.astype(o.dtype)


def _dft_inverse(y_re, y_im, g_re, g_im, u_arr, u_blk0, gate_arr, gate_blk0, bias, out_dtypes):
    bsz, nb, blk, ch = y_re.shape
    tc = 512
    nc = ch // tc
    whole = pl.BlockSpec((blk, blk), lambda bi, c, t: (0, 0))
    spec = pl.BlockSpec((None, None, blk, tc), lambda bi, c, t: (bi, t, 0, c))
    return pl.pallas_call(
        _dft_inv_kernel,
        grid=(bsz, nc, nb),
        in_specs=[whole, whole, spec, spec,
                  pl.BlockSpec((None, blk, tc), lambda bi, c, t: (bi, t, u_blk0 * nc + c)),
                  pl.BlockSpec((None, blk, tc), lambda bi, c, t: (bi, t, gate_blk0 * nc + c)),
                  pl.BlockSpec((1, tc), lambda bi, c, t: (0, c))],
        out_specs=[pl.BlockSpec((None, blk, tc), lambda bi, c, t: (bi, t, c))] * len(out_dtypes),
        out_shape=[jax.ShapeDtypeStruct((bsz, nb * blk, ch), dt) for dt in out_dtypes],
        compiler_params=_cparams(3),
        name="dft_inverse",
    )(g_re, g_im, y_re, y_im, u_arr, gate_arr, bias)


def _ffn(h, w_gate, w_up, w_down, lead, rows, side=None, side_args=()):
    n_tiles = pl.cdiv(D_FF, MM_TN)
    outs = _matmul_wres([h], [w_gate, w_up], [(0, 0), (0, 1)], _ep_swiglu, D_FF, [BF16], tm=MM_TM, tn=MM_TN,
                        m_tiles=rows // MM_TM, n_tiles=n_tiles, k=D_MODEL, ch=MM_CHUNK, lead=[lead, lead],
                        last_width=D_FF - (n_tiles - 1) * MM_TN, side=side, side_args=side_args,
                        name="ffn_gate_up")
    f = _matmul_wres([outs[0]], [w_down], [(0, 0)], _ep_store, D_MODEL, [BRANCH_DTYPE], tm=DOWN_TM, tn=MM_TN,
                     m_tiles=rows // DOWN_TM, n_tiles=D_MODEL // MM_TN, k=D_FF, ch=DOWN_CHUNK, lead=[lead],
                     name="ffn_down")[0]
    return f, (outs[1] if side is not None else None)


def kernel(x, c, ctx, c_ctx, w_ada, b_ada, pre_g, post_g, ffn_w_gate, ffn_w_up, ffn_w_down, w_in, q_norm_g,
           k_norm_g, short_w, short_b, filt_w1, filt_b1, filt_w2, filt_b2, filt_w3, filt_b3, filt_freq,
           filt_w_out, hyena_bias, w_br_attn, w_br_hyena, w_out):
    bsz, n_lat, d = x.shape
    n_ctx = ctx.shape[1]
    rows_x, rows_c = bsz * n_lat, bsz * n_ctx
    x2d = x.reshape(rows_x, d)
    c2d = ctx.reshape(rows_c, d)
    l = 0

    cond = jnp.concatenate([c, c_ctx[None, :], jnp.zeros((3, d), F32)], axis=0)
    n_first = 2 * d
    bias = b_ada[l][None, :]
    mods_first = _ada(cond, w_ada, bias, l, n_first)
    pre = [pre_g[l, s][None, :] for s in range(3)]
    post = [post_g[l, s][None, :] for s in range(3)]

    h0 = _modulate_first(x2d, c2d, mods_first[:5].reshape(5, 2, d), pre[0], shift_i=0, scale_i=1)
    side = _SidePlan(lead=(l,), k=d, ts=ADA_SIDE_TILE, n_tiles=(N_MOD * d - n_first) // ADA_SIDE_TILE,
                     col0=n_first)
    f1, mods_rest = _ffn(h0, ffn_w_gate, ffn_w_up, ffn_w_down, (l, 0), rows_x + rows_c, side=side,
                         side_args=(cond, w_ada, bias[:, n_first:]))
    mods = jnp.concatenate([mods_first, mods_rest], axis=1)[:5].reshape(5, N_MOD, d)
    x1, h1 = _residual(f1, x2d, c2d, mods, post[0], pre[1], gate_i=2, weight=MACARON_W, shift_i=3, scale_i=4)

    tm, tn = MM_TM, MM_TN
    p = _matmul_wres([h1], [w_in], [(0, 0)], _ep_store, IN_COLS, [F32], tm=tm, tn=IN_TN, m_tiles=rows_x // tm,
                     n_tiles=IN_COLS // IN_TN, k=d, ch=IN_CHUNK, lead=[(l,)], name="in_proj")[0]
    pc_kv = _matmul_wres([h1], [w_in], [(0, 0)], _ep_store, 2 * KV_WIDTH, [F32], tm=CTX_TM, tn=tn,
                         m_tiles=rows_c // CTX_TM, n_tiles=2 * KV_WIDTH // tn, k=d, ch=MM_CHUNK, lead=[(l,)],
                         col0=Q_END, a_blk0=rows_x // CTX_TM, name="in_proj_ctx")[0]

    tables = _rope_tables(n_lat)
    q16, k16, v16 = _qkv_prep(p, q_norm_g[l][None, :], k_norm_g[l][None, :], tables)
    kc16, vc16 = _kv_prep_ctx(pc_kv, k_norm_g[l][None, :])
    k_all = jnp.concatenate([kc16.reshape(bsz, n_ctx, KV_WIDTH), k16.reshape(bsz, n_lat, KV_WIDTH)], axis=1)
    v_all = jnp.concatenate([vc16.reshape(bsz, n_ctx, KV_WIDTH), v16.reshape(bsz, n_lat, KV_WIDTH)], axis=1)
    attn_o = _attention(q16, k_all, v_all, n_lat)

    assert n_lat == 2 * CONV_BLOCK
    f_re, f_im, g_re, g_im = _dft_matrices(CONV_BLOCK)
    h_time = _hyena_filters(n_lat, filt_w1[l], filt_b1[l], filt_w2[l], filt_b2[l], filt_w3[l], filt_b3[l],
                            filt_freq[l], filt_w_out[l])
    spectra = _filter_spectra(h_time, f_re, f_im)
    p3 = p.reshape(bsz, n_lat, IN_COLS)
    u, v16h = _short_conv(p3, short_w[l], short_b[l][None, :])
    y_re, y_im = _dft_forward(v16h, f_re, f_im, spectra, order=0)
    z, z16 = _dft_inverse(y_re, y_im, g_re, g_im, u, 0, u, 1, hyena_bias[l, 0][None, :], [F32, BF16])
    y_re, y_im = _dft_forward(z16, f_re, f_im, spectra, order=1)
    hy_o = _dft_inverse(y_re, y_im, g_re, g_im, z, 0, u, 2, hyena_bias[l, 1][None, :], [BF16])[0]

    gate_spec = lambda off: pl.BlockSpec((tm, tn), lambda j, i: (i, off + j))
    merged = _matmul_wres([attn_o, hy_o.reshape(rows_x, HYENA_WIDTH)], [w_br_attn, w_br_hyena], [(0, 0), (1, 1)],
                          _ep_merge, d, [BF16], tm=tm, tn=tn, m_tiles=rows_x // tm, n_tiles=d // tn,
                          k=ATTN_WIDTH, ch=MM_CHUNK, lead=[(l,), (l,)], extras=(p, p),
                          extra_specs=(gate_spec(HY_END // tn), gate_spec((HY_END + d) // tn)), name="merge")[0]
    out = _matmul_wres([merged], [w_out], [(0, 0)], _ep_store, d, [BRANCH_DTYPE], tm=tm, tn=IN_TN,
                       m_tiles=rows_x // tm, n_tiles=d // IN_TN, k=d, ch=IN_CHUNK, lead=[(l,)], name="out_proj")[0]
    x2, h2 = _residual(out, x1, None, mods, post[1], pre[2], gate_i=5, weight=1.0, shift_i=6, scale_i=7)

    f2, _ = _ffn(h2, ffn_w_gate, ffn_w_up, ffn_w_down, (l, 1), rows_x)
    x3 = _residual(f2, x2, None, mods, post[2], None, gate_i=8, weight=MACARON_W)[0]
    return x3.reshape(bsz, n_lat, d)
```

```python
import functools
import math
from typing import NamedTuple

import jax
import jax.numpy as jnp
from jax import lax
from jax.experimental import pallas as pl
from jax.experimental.pallas import tpu as pltpu

F32 = jnp.float32
BF16 = jnp.bfloat16
BRANCH_DTYPE = BF16

D_MODEL = 4096
GRID_W = 64
HEAD_DIM = 128
N_Q_HEADS = 16
N_KV_HEADS = 4
Q_PER_KV = N_Q_HEADS // N_KV_HEADS
ATTN_WIDTH = N_Q_HEADS * HEAD_DIM
KV_WIDTH = N_KV_HEADS * HEAD_DIM
ROPE_THETA = 10000.0
AXIS_DIM = HEAD_DIM // 2
HYENA_WIDTH = D_MODEL // 2
FILTER_EMB = 33
FILTER_HIDDEN = 64
DECAY_TARGET = 1e-2
FAST_DECAY_PCT = 0.3
SLOW_DECAY_PCT = 1.5
D_FF = 11008
MACARON_W = 0.5
N_MOD = 9
NORM_EPS = 1e-6
Q_END = ATTN_WIDTH
K_END = Q_END + KV_WIDTH
V_END = K_END + KV_WIDTH
HY_END = V_END + 3 * HYENA_WIDTH
IN_COLS = HY_END + 2 * D_MODEL
SCORE_SCALE_LOG2 = math.log2(math.e) / math.sqrt(HEAD_DIM)

V7X_LANES = 128
BF16_SUBLANES = 16
V7X_VMEM_LIMIT_BYTES = 56 * 1024 * 1024

MM_TM = 1024
MM_TN = 512
MM_CHUNK = 1024
IN_TN = 1024
IN_CHUNK = 512
CTX_TM = 256
DOWN_TM = 512
DOWN_CHUNK = D_FF // 16
ROW_TILE = 256
ADA_SIDE_TILE = 256
CONV_BLOCK = 1024
DFT_ROWS = 64
FILT_PAD = V7X_LANES


def _cparams(n_axes):
    return pltpu.CompilerParams(dimension_semantics=("arbitrary",) * n_axes,
                                vmem_limit_bytes=V7X_VMEM_LIMIT_BYTES)


def _rms(x):
    return x * lax.rsqrt(jnp.mean(x * x, axis=-1, keepdims=True) + NORM_EPS)


def _sigmoid(x):
    return 0.5 * jnp.tanh(0.5 * x) + 0.5


def _ada_kernel(c_ref, w_ref, b_ref, o_ref):
    c = c_ref[...]
    s = (c * jax.nn.sigmoid(c)).astype(BF16)
    o_ref[...] = jnp.dot(s, w_ref[...].astype(BF16), preferred_element_type=F32) + b_ref[...]


def _ada(cond, w, b, layer, n):
    tn = 512
    return pl.pallas_call(
        _ada_kernel,
        grid=(n // tn,),
        in_specs=[pl.BlockSpec((8, D_MODEL), lambda j: (0, 0)),
                  pl.BlockSpec((None, D_MODEL, tn), lambda j: (layer, 0, j)),
                  pl.BlockSpec((1, tn), lambda j: (0, j))],
        out_specs=pl.BlockSpec((8, tn), lambda j: (0, j)),
        out_shape=jax.ShapeDtypeStruct((8, n), F32),
        compiler_params=_cparams(1),
        name="ada",
    )(cond, w, b)


def _mod_index(i, rows_per_batch_tiles):
    return jnp.minimum(i // rows_per_batch_tiles, 4)


def _modulate_kernel(x_ref, c_ref, m_ref, g_ref, o_ref, *, nx, shift_i, scale_i):
    i = pl.program_id(0)

    def body(src):
        y = _rms(src[...]) * g_ref[...]
        o_ref[...] = (y * (1.0 + m_ref[0, scale_i:scale_i + 1, :])
                      + m_ref[0, shift_i:shift_i + 1, :]).astype(o_ref.dtype)

    pl.when(i < nx)(lambda: body(x_ref))
    pl.when(i >= nx)(lambda: body(c_ref))


def _modulate_first(x2d, c2d, mods, g, shift_i, scale_i):
    tr = ROW_TILE
    nx, nc = x2d.shape[0] // tr, c2d.shape[0] // tr
    per_batch = (x2d.shape[0] // 4) // tr
    return pl.pallas_call(
        functools.partial(_modulate_kernel, nx=nx, shift_i=shift_i, scale_i=scale_i),
        grid=(nx + nc,),
        in_specs=[pl.BlockSpec((tr, D_MODEL), lambda i: (jnp.minimum(i, nx - 1), 0)),
                  pl.BlockSpec((tr, D_MODEL), lambda i: (jnp.maximum(i - nx, 0), 0)),
                  pl.BlockSpec((1, mods.shape[1], D_MODEL), lambda i: (_mod_index(i, per_batch), 0, 0)),
                  pl.BlockSpec((1, D_MODEL), lambda i: (0, 0))],
        out_specs=pl.BlockSpec((tr, D_MODEL), lambda i: (i, 0)),
        out_shape=jax.ShapeDtypeStruct(((nx + nc) * tr, D_MODEL), BF16),
        compiler_params=_cparams(1),
        name="modulate_first",
    )(x2d, c2d, mods, g)


def _resid_kernel(*refs, nx, two_source, gate_i, weight, shift_i, scale_i, with_next):
    it = iter(refs)
    f_ref = next(it)
    x_ref = next(it)
    c_ref = next(it) if two_source else None
    m_ref = next(it)
    pg_ref = next(it)
    ng_ref = next(it) if with_next else None
    xo_ref = next(it)
    ho_ref = next(it) if with_next else None
    i = pl.program_id(0)

    def body(src):
        y = _rms(f_ref[...].astype(F32)) * pg_ref[...]
        xn = src[...] + (weight * m_ref[0, gate_i:gate_i + 1, :]) * y
        xo_ref[...] = xn
        if with_next:
            h = _rms(xn) * ng_ref[...]
            ho_ref[...] = (h * (1.0 + m_ref[0, scale_i:scale_i + 1, :])
                           + m_ref[0, shift_i:shift_i + 1, :]).astype(ho_ref.dtype)

    if two_source:
        pl.when(i < nx)(lambda: body(x_ref))
        pl.when(i >= nx)(lambda: body(c_ref))
    else:
        body(x_ref)


def _residual(f, x2d, c2d, mods, post_g, next_g, *, gate_i, weight, shift_i=0, scale_i=0):
    tr = ROW_TILE
    rows = f.shape[0]
    n = rows // tr
    two_source = c2d is not None
    with_next = next_g is not None
    nx = x2d.shape[0] // tr if two_source else n
    per_batch = (8192 // 4) // tr
    row = pl.BlockSpec((tr, D_MODEL), lambda i: (i, 0))
    vec = pl.BlockSpec((1, D_MODEL), lambda i: (0, 0))
    in_specs = [row]
    args = [f]
    if two_source:
        in_specs += [pl.BlockSpec((tr, D_MODEL), lambda i: (jnp.minimum(i, nx - 1), 0)),
                     pl.BlockSpec((tr, D_MODEL), lambda i: (jnp.maximum(i - nx, 0), 0))]
        args += [x2d, c2d]
    else:
        in_specs += [row]
        args += [x2d]
    in_specs += [pl.BlockSpec((1, N_MOD, D_MODEL), lambda i: (_mod_index(i, per_batch), 0, 0)), vec]
    args += [mods, post_g]
    out_specs = [row]
    out_shape = [jax.ShapeDtypeStruct((rows, D_MODEL), F32)]
    if with_next:
        in_specs += [vec]
        args += [next_g]
        out_specs += [row]
        out_shape += [jax.ShapeDtypeStruct((rows, D_MODEL), BF16)]
    return pl.pallas_call(
        functools.partial(_resid_kernel, nx=nx, two_source=two_source, gate_i=gate_i, weight=weight,
                          shift_i=shift_i, scale_i=scale_i, with_next=with_next),
        grid=(n,),
        in_specs=in_specs,
        out_specs=out_specs,
        out_shape=out_shape,
        compiler_params=_cparams(1),
        name="residual",
    )(*args)


class _WeightPlan(NamedTuple):
    k: int
    ch: int
    tn: int
    n_tiles: int
    m_tiles: int
    col0: int
    last_width: int
    lead: tuple


def _wres_kernel(*refs, n_a, n_w, pairs, n_extra, n_out, epilogue, plan, side):
    it = iter(refs)
    a_refs = [next(it) for _ in range(n_a)]
    w_refs = [next(it) for _ in range(n_w)]
    e_refs = [next(it) for _ in range(n_extra)]
    if side is not None:
        side_c_ref, side_w_ref, side_b_ref = next(it), next(it), next(it)
    o_refs = [next(it) for _ in range(n_out)]
    if side is not None:
        side_o_ref = next(it)
    wbf, stage, sem = next(it), next(it), next(it)
    if side is not None:
        side_stage, side_sem = next(it), next(it)
    j = pl.program_id(0)
    i = pl.program_id(1)
    slot = j % 2
    per_w = plan.k // plan.ch
    n_chunks = n_w * per_w
    ragged = plan.last_width != plan.tn

    def chunk_copy(tile, c, width):
        w, r = divmod(c, per_w)
        col = plan.col0 + tile * plan.tn
        if not isinstance(col, int):
            col = pl.multiple_of(col, V7X_LANES)
        src = w_refs[w].at[(*plan.lead[w], pl.ds(r * plan.ch, plan.ch), pl.ds(col, width))]
        return pltpu.make_async_copy(src, stage.at[c % 2, :, pl.ds(0, width)], sem.at[c % 2])

    def width_variants(tile):
        if not ragged:
            return [(plan.tn, tile < plan.n_tiles)]
        return [(plan.tn, tile < plan.n_tiles - 1), (plan.last_width, tile == plan.n_tiles - 1)]

    @pl.when(jnp.logical_and(j == 0, i == 0))
    def _():
        chunk_copy(0, 0, plan.tn).start()
        for c in range(n_chunks):
            w, r = divmod(c, per_w)
            chunk_copy(0, c, plan.tn).wait()
            if c + 1 < n_chunks:
                chunk_copy(0, c + 1, plan.tn).start()
            wbf[0, w, pl.ds(r * plan.ch, plan.ch), :] = stage[c % 2].astype(BF16)
        second = plan.last_width if plan.n_tiles == 2 else plan.tn
        chunk_copy(1, 0, second).start()

    for width, cond in width_variants(j + 1):
        for c in range(n_chunks):
            @pl.when(jnp.logical_and(cond, i == c))
            def _(width=width, c=c):
                chunk_copy(j + 1, c, width).wait()
                if c + 1 < n_chunks:
                    chunk_copy(j + 1, c + 1, width).start()

    for width, cond in width_variants(j + 2):
        @pl.when(jnp.logical_and(cond, i == plan.m_tiles - 1))
        def _(width=width):
            chunk_copy(j + 2, 0, width).start()

    if side is not None:
        _side_stream_dma(side, side_w_ref, side_stage, side_sem, j * plan.m_tiles + i)

    def compute(width):
        c_now = jnp.minimum(i, n_chunks - 1)
        w_now = c_now // per_w
        row_now = pl.multiple_of((c_now % per_w) * plan.ch, BF16_SUBLANES)
        wbf[1 - slot, w_now, pl.ds(row_now, plan.ch), :] = stage[c_now % 2].astype(BF16)
        if side is not None:
            _side_stream_compute(side, side_c_ref, side_b_ref, side_o_ref, side_stage, j * plan.m_tiles + i)
        prods = [jnp.dot(a_refs[ai][...], wbf[slot, wi, :, pl.ds(0, width)], preferred_element_type=F32)
                 for ai, wi in pairs]
        epilogue(prods, e_refs, o_refs, width)

    if ragged:
        pl.when(j < plan.n_tiles - 1)(lambda: compute(plan.tn))
        pl.when(j == plan.n_tiles - 1)(lambda: compute(plan.last_width))
    else:
        compute(plan.tn)


class _SidePlan(NamedTuple):
    lead: tuple
    k: int
    ts: int
    n_tiles: int
    col0: int


def _side_stream_dma(side, w_ref, stage, sem, g):
    def copy(tile, slot):
        col = pl.multiple_of(side.col0 + tile * side.ts, V7X_LANES)
        return pltpu.make_async_copy(w_ref.at[(*side.lead, slice(None), pl.ds(col, side.ts))],
                                     stage.at[slot], sem.at[slot])

    pl.when(g == 0)(lambda: copy(g, 0).start())
    for slot in range(2):
        pl.when(jnp.logical_and(g < side.n_tiles, g % 2 == slot))(lambda slot=slot: copy(g, slot).wait())
    for slot in range(2):
        pl.when(jnp.logical_and(g + 1 < side.n_tiles, (g + 1) % 2 == slot))(
            lambda slot=slot: copy(g + 1, slot).start())


def _side_stream_compute(side, c_ref, b_ref, o_ref, stage, g):
    t = jnp.minimum(g, side.n_tiles - 1)
    c = c_ref[...]
    s = (c * jax.nn.sigmoid(c)).astype(BF16)
    o_ref[...] = jnp.dot(s, stage[t % 2].astype(BF16), preferred_element_type=F32) + b_ref[...]


def _matmul_wres(a_list, w_list, pairs, epilogue, out_cols, out_dtypes, *, tm, tn, m_tiles, n_tiles, k, ch,
                 lead, col0=0, last_width=None, a_blk0=0, extras=(), extra_specs=(), side=None, side_args=(),
                 name="matmul"):
    plan = _WeightPlan(k=k, ch=ch, tn=tn, n_tiles=n_tiles, m_tiles=m_tiles, col0=col0,
                       last_width=last_width or tn, lead=tuple(lead))
    n_chunks = len(w_list) * (k // ch)
    assert n_chunks <= m_tiles and n_chunks % 2 == 0 and n_tiles >= 2 and k % ch == 0
    in_specs = [pl.BlockSpec((tm, k), lambda j, i: (i + a_blk0, 0)) for _ in a_list]
    in_specs += [pl.BlockSpec(memory_space=pl.ANY) for _ in w_list]
    in_specs += list(extra_specs)
    out_specs = [pl.BlockSpec((tm, tn), lambda j, i: (i, j)) for _ in out_dtypes]
    out_shape = [jax.ShapeDtypeStruct((m_tiles * tm, out_cols), dt) for dt in out_dtypes]
    scratch = [pltpu.VMEM((2, len(w_list), k, tn), BF16), pltpu.VMEM((2, ch, tn), F32),
               pltpu.SemaphoreType.DMA((2,))]
    if side is not None:
        assert side.n_tiles <= n_tiles * m_tiles
        side_tile = lambda j, i: (0, jnp.minimum(j * m_tiles + i, side.n_tiles - 1))
        in_specs += [pl.BlockSpec((8, side.k), lambda j, i: (0, 0)), pl.BlockSpec(memory_space=pl.ANY),
                     pl.BlockSpec((1, side.ts), side_tile)]
        out_specs += [pl.BlockSpec((8, side.ts), side_tile)]
        out_shape += [jax.ShapeDtypeStruct((8, side.n_tiles * side.ts), F32)]
        scratch += [pltpu.VMEM((2, side.k, side.ts), F32), pltpu.SemaphoreType.DMA((2,))]
    return pl.pallas_call(
        functools.partial(_wres_kernel, n_a=len(a_list), n_w=len(w_list), pairs=tuple(pairs),
                          n_extra=len(extras), n_out=len(out_dtypes), epilogue=epilogue, plan=plan, side=side),
        grid=(n_tiles, m_tiles),
        in_specs=in_specs,
        out_specs=out_specs,
        out_shape=out_shape,
        scratch_shapes=scratch,
        compiler_params=_cparams(2),
        name=name,
    )(*a_list, *w_list, *extras, *side_args)


def _ep_store(accs, e_refs, o_refs, width):
    o_refs[0][:, pl.ds(0, width)] = accs[0].astype(o_refs[0].dtype)


def _ep_swiglu(accs, e_refs, o_refs, width):
    g, u = accs
    o_refs[0][:, pl.ds(0, width)] = (g * _sigmoid(g) * u).astype(o_refs[0].dtype)


def _ep_merge(accs, e_refs, o_refs, width):
    a, h = accs
    ga, gh = e_refs
    o_refs[0][...] = (_sigmoid(ga[...]) * a + _sigmoid(gh[...]) * h).astype(o_refs[0].dtype)


def _rope_tables(n_lat):
    t = jnp.arange(n_lat, dtype=jnp.int32)
    row = (t // GRID_W).astype(F32)
    col = (t % GRID_W).astype(F32)
    inv_freq = ROPE_THETA ** (-jnp.arange(0, AXIS_DIM, 2, dtype=F32) / AXIS_DIM)
    ang = jnp.concatenate([jnp.tile(row[:, None] * inv_freq, (1, 2)),
                           jnp.tile(col[:, None] * inv_freq, (1, 2))], axis=-1)
    lane = jnp.arange(HEAD_DIM, dtype=jnp.int32)
    first = (lane % AXIS_DIM) < (AXIS_DIM // 2)
    cos = jnp.cos(ang)
    sin = jnp.sin(ang)
    sin_a = jnp.where(first, -sin, 0.0)
    sin_b = jnp.where(first, 0.0, sin)
    return cos, sin_a, sin_b


def _head_prep(x, g, cos, sin_a, sin_b, rope):
    y = _rms(x) * g
    if rope:
        up = pltpu.roll(y, HEAD_DIM - AXIS_DIM // 2, axis=1)
        dn = pltpu.roll(y, AXIS_DIM // 2, axis=1)
        y = y * cos + up * sin_a + dn * sin_b
    return y


def _qkv_prep_kernel(q_ref, kv_ref, qg_ref, kg_ref, cos_ref, sa_ref, sb_ref, qo_ref, ko_ref, vo_ref):
    cos, sa, sb = cos_ref[...], sa_ref[...], sb_ref[...]
    for h in range(N_Q_HEADS):
        sl = slice(h * HEAD_DIM, (h + 1) * HEAD_DIM)
        q = _head_prep(q_ref[:, sl], qg_ref[...], cos, sa, sb, True)
        qo_ref[:, sl] = (q * SCORE_SCALE_LOG2).astype(qo_ref.dtype)
    for h in range(N_KV_HEADS):
        sl = slice(h * HEAD_DIM, (h + 1) * HEAD_DIM)
        ko_ref[:, sl] = _head_prep(kv_ref[:, sl], kg_ref[...], cos, sa, sb, True).astype(ko_ref.dtype)
    vo_ref[...] = kv_ref[:, KV_WIDTH:].astype(vo_ref.dtype)


def _qkv_prep(p, q_g, k_g, tables):
    tr = ROW_TILE
    rows = p.shape[0]
    cos, sin_a, sin_b = tables
    nt = cos.shape[0] // tr
    tab = pl.BlockSpec((tr, HEAD_DIM), lambda i: (i % nt, 0))
    vec = pl.BlockSpec((1, HEAD_DIM), lambda i: (0, 0))
    kv_out = pl.BlockSpec((tr, KV_WIDTH), lambda i: (i, 0))
    return pl.pallas_call(
        _qkv_prep_kernel,
        grid=(rows // tr,),
        in_specs=[pl.BlockSpec((tr, ATTN_WIDTH), lambda i: (i, 0)),
                  pl.BlockSpec((tr, 2 * KV_WIDTH), lambda i: (i, Q_END // (2 * KV_WIDTH))),
                  vec, vec, tab, tab, tab],
        out_specs=[pl.BlockSpec((tr, ATTN_WIDTH), lambda i: (i, 0)), kv_out, kv_out],
        out_shape=[jax.ShapeDtypeStruct((rows, ATTN_WIDTH), BF16),
                   jax.ShapeDtypeStruct((rows, KV_WIDTH), BF16),
                   jax.ShapeDtypeStruct((rows, KV_WIDTH), BF16)],
        compiler_params=_cparams(1),
        name="qkv_prep",
    )(p, p, q_g, k_g, cos, sin_a, sin_b)


def _kv_prep_ctx_kernel(kv_ref, kg_ref, ko_ref, vo_ref):
    for h in range(N_KV_HEADS):
        sl = slice(h * HEAD_DIM, (h + 1) * HEAD_DIM)
        ko_ref[:, sl] = _head_prep(kv_ref[:, sl], kg_ref[...], None, None, None, False).astype(ko_ref.dtype)
    vo_ref[...] = kv_ref[:, KV_WIDTH:].astype(vo_ref.dtype)


def _kv_prep_ctx(pc_kv, k_g):
    tr = ROW_TILE
    rows = pc_kv.shape[0]
    kv_out = pl.BlockSpec((tr, KV_WIDTH), lambda i: (i, 0))
    return pl.pallas_call(
        _kv_prep_ctx_kernel,
        grid=(rows // tr,),
        in_specs=[pl.BlockSpec((tr, 2 * KV_WIDTH), lambda i: (i, 0)),
                  pl.BlockSpec((1, HEAD_DIM), lambda i: (0, 0))],
        out_specs=[kv_out, kv_out],
        out_shape=[jax.ShapeDtypeStruct((rows, KV_WIDTH), BF16)] * 2,
        compiler_params=_cparams(1),
        name="kv_prep_ctx",
    )(pc_kv, k_g)


def _attn_kernel(q_ref, k_ref, v_ref, o_ref):
    k = k_ref[...]
    v = v_ref[...]
    for g in range(Q_PER_KV):
        sl = slice(g * HEAD_DIM, (g + 1) * HEAD_DIM)
        s = lax.dot_general(q_ref[:, sl], k, (((1,), (1,)), ((), ())), preferred_element_type=F32)
        m = jnp.max(s, axis=-1, keepdims=True)
        p = jnp.exp2(s - m)
        l = jnp.sum(p, axis=-1, keepdims=True)
        o = jnp.dot(p.astype(BF16), v, preferred_element_type=F32)
        o_ref[:, sl] = (o / l).astype(o_ref.dtype)


def _attention(q, k_all, v_all, n_lat):
    tq = 512
    b, t, _ = k_all.shape
    nq = n_lat // tq
    gw = Q_PER_KV * HEAD_DIM
    return pl.pallas_call(
        _attn_kernel,
        grid=(b, N_KV_HEADS, nq),
        in_specs=[pl.BlockSpec((tq, gw), lambda bi, h, i: (bi * nq + i, h)),
                  pl.BlockSpec((None, t, HEAD_DIM), lambda bi, h, i: (bi, 0, h)),
                  pl.BlockSpec((None, t, HEAD_DIM), lambda bi, h, i: (bi, 0, h))],
        out_specs=pl.BlockSpec((tq, gw), lambda bi, h, i: (bi * nq + i, h)),
        out_shape=jax.ShapeDtypeStruct(q.shape, BF16),
        compiler_params=_cparams(3),
        name="attention",
    )(q, k_all, v_all)


def _filter_kernel(z_ref, w1_ref, b1_ref, w2_ref, b2_ref, w3_ref, b3_ref, fr_ref, wo_ref, t_ref, ad_ref, o_ref,
                   a_ref):
    j = pl.program_id(0)
    hi = lax.Precision.HIGHEST

    @pl.when(j == 0)
    def _():
        fr = fr_ref[...]
        a = jnp.sin(fr * (jnp.dot(z_ref[...], w1_ref[...], precision=hi, preferred_element_type=F32)
                          + b1_ref[...]))
        a = jnp.sin(fr * (jnp.dot(a, w2_ref[...], precision=hi, preferred_element_type=F32) + b2_ref[...]))
        a_ref[...] = jnp.sin(fr * (jnp.dot(a, w3_ref[...], precision=hi, preferred_element_type=F32)
                                   + b3_ref[...]))

    h = jnp.dot(a_ref[...].astype(BF16), wo_ref[...].astype(BF16), preferred_element_type=F32)
    h = h * jnp.exp(-t_ref[...] * ad_ref[...])
    backward = (j // 2) % 2 == 1
    row = lax.broadcasted_iota(jnp.int32, h.shape, 0)
    o_ref[...] = jnp.where(jnp.logical_and(backward, row == 0), 0.0, h)


def _pad2(a, rows, cols):
    return jnp.pad(a, ((0, rows - a.shape[0]), (0, cols - a.shape[1])))


def _hyena_filters(n, w1, b1, w2, b2, w3, b3, freq, w_out):
    t = jnp.linspace(0.0, 1.0, n, dtype=F32)[:, None]
    bands = (FILTER_EMB - 1) // 2
    w = 2.0 * math.pi * jnp.arange(n, dtype=F32)[:, None] / n
    f = jnp.linspace(1e-4, bands - 1, bands, dtype=F32)[None, :]
    z = jnp.concatenate([t, jnp.cos(f * w), -jnp.sin(f * w)], axis=-1)
    max_decay = math.log(DECAY_TARGET) / FAST_DECAY_PCT
    min_decay = math.log(DECAY_TARGET) / SLOW_DECAY_PCT
    ad = jnp.abs(jnp.linspace(min_decay, max_decay, HYENA_WIDTH, dtype=F32))[None, :]
    p = FILT_PAD
    tn = 1024
    n_out = w_out.shape[1]
    full = lambda shape: pl.BlockSpec(shape, lambda j: (0, 0))
    return pl.pallas_call(
        _filter_kernel,
        grid=(n_out // tn,),
        in_specs=[full((n, p)), full((p, p)), full((1, p)), full((p, p)), full((1, p)), full((p, p)),
                  full((1, p)), full((1, p)),
                  pl.BlockSpec((p, tn), lambda j: (0, j)),
                  full((n, 1)),
                  pl.BlockSpec((1, tn), lambda j: (0, j % (HYENA_WIDTH // tn)))],
        out_specs=pl.BlockSpec((n, tn), lambda j: (0, j)),
        out_shape=jax.ShapeDtypeStruct((n, n_out), F32),
        scratch_shapes=[pltpu.VMEM((n, p), F32)],
        compiler_params=_cparams(1),
        name="hyena_filters",
    )(_pad2(z, n, p), _pad2(w1, p, p), _pad2(b1[None, :], 1, p), _pad2(w2, p, p), _pad2(b2[None, :], 1, p),
      _pad2(w3, p, p), _pad2(b3[None, :], 1, p), _pad2(freq[None, :], 1, p), _pad2(w_out, p, n_out), t, ad)


def _dft_matrix_kernel(ca_ref, sa_ref, cb_ref, sb_ref, alt_ref, cs_ref, fre_ref, fim_ref, gre_ref, gim_ref):
    i = pl.program_id(0)
    ca, sa = ca_ref[...], sa_ref[...]
    cb, sb = cb_ref[...], sb_ref[...]
    cos = ca * cb - sa * sb
    sin = sa * cb + ca * sb
    cs = cs_ref[...]
    row = lax.broadcasted_iota(jnp.int32, cos.shape, 0)
    col = lax.broadcasted_iota(jnp.int32, cos.shape, 1)
    nyquist_row = jnp.logical_and(i == 0, row == 0)
    alt_rows = jnp.where(row % 2 == 0, 1.0, -1.0)
    fre_ref[...] = cos.astype(fre_ref.dtype)
    fim_ref[...] = jnp.where(nyquist_row, alt_ref[...], -sin).astype(fim_ref.dtype)
    gre_ref[...] = (cos * cs).astype(gre_ref.dtype)
    gim_ref[...] = (jnp.where(col == 0, alt_rows, -sin) * cs).astype(gim_ref.dtype)


def _dft_matrices(n):
    big = 2 * n
    rb = DFT_ROWS
    t = jnp.arange(n, dtype=jnp.int32)[None, :]
    unit = 2.0 * math.pi / big
    coarse = ((rb * jnp.arange(n // rb, dtype=jnp.int32)[:, None] * t) % big).astype(F32) * unit
    fine = ((jnp.arange(rb, dtype=jnp.int32)[:, None] * t) % big).astype(F32) * unit
    alt = jnp.where(t % 2 == 0, 1.0, -1.0).astype(F32)
    cs = jnp.where(t == 0, 1.0 / big, 2.0 / big).astype(F32)
    coarse_spec = pl.BlockSpec((None, 1, n), lambda i: (i, 0, 0))
    whole = lambda r: pl.BlockSpec((r, n), lambda i: (0, 0))
    out_spec = pl.BlockSpec((rb, n), lambda i: (i, 0))
    return pl.pallas_call(
        _dft_matrix_kernel,
        grid=(n // rb,),
        in_specs=[coarse_spec, coarse_spec, whole(rb), whole(rb), whole(1), whole(1)],
        out_specs=[out_spec] * 4,
        out_shape=[jax.ShapeDtypeStruct((n, n), BF16)] * 4,
        compiler_params=_cparams(1),
        name="dft_matrices",
    )(jnp.cos(coarse)[:, None, :], jnp.sin(coarse)[:, None, :], jnp.cos(fine), jnp.sin(fine), alt, cs)


def _spectrum_kernel(fre_ref, fim_ref, flo_ref, fhi_ref, blo_ref, bhi_ref, k0r_ref, k0i_ref, k1r_ref, k1i_ref,
                     kmr_ref, kmi_ref):
    fre = fre_ref[...]
    fim = fim_ref[...]

    def spectrum(x_ref):
        x = x_ref[...].astype(BF16)
        return jnp.dot(fre, x, preferred_element_type=F32), jnp.dot(fim, x, preferred_element_type=F32)

    alr, ali = spectrum(flo_ref)
    ahr, ahi = spectrum(fhi_ref)
    blr, bli = spectrum(blo_ref)
    bhr, bhi = spectrum(bhi_ref)
    h0 = flo_ref[0:1, :].astype(BF16).astype(F32)
    row = lax.broadcasted_iota(jnp.int32, alr.shape, 0)
    first = row == 0
    sgn = jnp.where(row % 2 == 0, 1.0, -1.0)
    k0r_ref[...] = alr + blr
    k0i_ref[...] = jnp.where(first, ali + bli, ali - bli)
    k1r_ref[...] = ahr + sgn * (alr - h0)
    k1i_ref[...] = jnp.where(first, ahi + ali - h0, ahi + sgn * ali)
    kmr_ref[...] = bhr + sgn * blr
    kmi_ref[...] = jnp.where(first, bhi + bli, -bhi - sgn * bli)


def _filter_spectra(h_time, f_re, f_im):
    blk = CONV_BLOCK
    tc = 256
    per = HYENA_WIDTH // tc
    whole = pl.BlockSpec((blk, blk), lambda j: (0, 0))
    taps = lambda half, direction: pl.BlockSpec(
        (blk, tc), lambda j: (half, (j // per) * 2 * per + direction * per + j % per))
    return pl.pallas_call(
        _spectrum_kernel,
        grid=(2 * per,),
        in_specs=[whole, whole, taps(0, 0), taps(1, 0), taps(0, 1), taps(1, 1)],
        out_specs=[pl.BlockSpec((blk, tc), lambda j: (0, j))] * 6,
        out_shape=[jax.ShapeDtypeStruct((blk, 2 * HYENA_WIDTH), F32)] * 6,
        compiler_params=_cparams(1),
        name="filter_spectra",
    )(f_re, f_im, h_time, h_time, h_time, h_time)


def _short_conv_kernel(p_ref, w_ref, b_ref, u_ref, v16_ref):
    j = pl.program_id(1)
    x = p_ref[...]
    n = x.shape[0]
    row = lax.broadcasted_iota(jnp.int32, x.shape, 0)
    prev = jnp.where(row == 0, 0.0, pltpu.roll(x, 1, axis=0))
    nxt = jnp.where(row == n - 1, 0.0, pltpu.roll(x, n - 1, axis=0))
    u = prev * w_ref[0:1, :] + x * w_ref[1:2, :] + nxt * w_ref[2:3, :] + b_ref[...]
    u_ref[...] = u

    @pl.when(j < HYENA_WIDTH // u.shape[1])
    def _():
        v16_ref[...] = u.astype(v16_ref.dtype)


def _short_conv(p3, w, b):
    bsz, n, _ = p3.shape
    tc = 512
    nj = 3 * HYENA_WIDTH // tc
    nv = HYENA_WIDTH // tc
    off = V_END // tc
    return pl.pallas_call(
        _short_conv_kernel,
        grid=(bsz, nj),
        in_specs=[pl.BlockSpec((None, n, tc), lambda bi, j: (bi, 0, off + j)),
                  pl.BlockSpec((3, tc), lambda bi, j: (0, j)),
                  pl.BlockSpec((1, tc), lambda bi, j: (0, j))],
        out_specs=[pl.BlockSpec((None, n, tc), lambda bi, j: (bi, 0, j)),
                   pl.BlockSpec((None, n, tc), lambda bi, j: (bi, 0, jnp.minimum(j, nv - 1)))],
        out_shape=[jax.ShapeDtypeStruct((bsz, n, 3 * HYENA_WIDTH), F32),
                   jax.ShapeDtypeStruct((bsz, n, HYENA_WIDTH), BF16)],
        compiler_params=_cparams(2),
        name="short_conv",
    )(p3, w, b)


def _packed_cmul(xr, xi, kr, ki, first):
    return (jnp.where(first, xr * kr, xr * kr - xi * ki), jnp.where(first, xi * ki, xr * ki + xi * kr))


def _dft_fwd_kernel(fre_ref, fim_ref, u_ref, k0r_ref, k0i_ref, k1r_ref, k1i_ref, kmr_ref, kmi_ref, yre_ref,
                    yim_ref):
    fre = fre_ref[...]
    fim = fim_ref[...]
    blk = fre.shape[0]
    u0 = u_ref[pl.ds(0, blk), :]
    u1 = u_ref[pl.ds(blk, blk), :]
    x0r = jnp.dot(fre, u0, preferred_element_type=F32)
    x0i = jnp.dot(fim, u0, preferred_element_type=F32)
    x1r = jnp.dot(fre, u1, preferred_element_type=F32)
    x1i = jnp.dot(fim, u1, preferred_element_type=F32)
    first = lax.broadcasted_iota(jnp.int32, x0r.shape, 0) == 0
    k0r, k0i = k0r_ref[...], k0i_ref[...]
    ar, ai = _packed_cmul(x0r, x0i, k0r, k0i, first)
    br, bi = _packed_cmul(x1r, x1i, kmr_ref[...], kmi_ref[...], first)
    yre_ref[0] = (ar + br).astype(yre_ref.dtype)
    yim_ref[0] = (ai + bi).astype(yim_ref.dtype)
    cr, ci = _packed_cmul(x0r, x0i, k1r_ref[...], k1i_ref[...], first)
    dr, di = _packed_cmul(x1r, x1i, k0r, k0i, first)
    yre_ref[1] = (cr + dr).astype(yre_ref.dtype)
    yim_ref[1] = (ci + di).astype(yim_ref.dtype)


def _dft_forward(u16, f_re, f_im, spectra, order):
    bsz, n, ch = u16.shape
    blk = CONV_BLOCK
    tc = 256
    nc = ch // tc
    whole = pl.BlockSpec((blk, blk), lambda bi, c: (0, 0))
    seg = pl.BlockSpec((blk, tc), lambda bi, c: (0, order * nc + c))
    return pl.pallas_call(
        _dft_fwd_kernel,
        grid=(bsz, nc),
        in_specs=[whole, whole, pl.BlockSpec((None, n, tc), lambda bi, c: (bi, 0, c))] + [seg] * 6,
        out_specs=[pl.BlockSpec((None, 2, blk, tc), lambda bi, c: (bi, 0, 0, c))] * 2,
        out_shape=[jax.ShapeDtypeStruct((bsz, 2, blk, ch), BF16)] * 2,
        compiler_params=_cparams(2),
        name="dft_forward",
    )(f_re, f_im, u16, *spectra)


def _dft_inv_kernel(gre_ref, gim_ref, yre_ref, yim_ref, u_ref, gate_ref, bias_ref, *o_refs):
    y = (jnp.dot(gre_ref[...], yre_ref[...], preferred_element_type=F32)
         + jnp.dot(gim_ref[...], yim_ref[...], preferred_element_type=F32))
    z = gate_ref[...] * (y + u_ref[...] * bias_ref[...])
    for o in o_refs:
        o[...] = z.astype(o.dtype)


def _dft_inverse(y_re, y_im, g_re, g_im, u_arr, u_blk0, gate_arr, gate_blk0, bias, out_dtypes):
    bsz, nb, blk, ch = y_re.shape
    tc = 512
    nc = ch // tc
    whole = pl.BlockSpec((blk, blk), lambda bi, c, t: (0, 0))
    spec = pl.BlockSpec((None, None, blk, tc), lambda bi, c, t: (bi, t, 0, c))
    return pl.pallas_call(
        _dft_inv_kernel,
        grid=(bsz, nc, nb),
        in_specs=[whole, whole, spec, spec,
                  pl.BlockSpec((None, blk, tc), lambda bi, c, t: (bi, t, u_blk0 * nc + c)),
                  pl.BlockSpec((None, blk, tc), lambda bi, c, t: (bi, t, gate_blk0 * nc + c)),
                  pl.BlockSpec((1, tc), lambda bi, c, t: (0, c))],
        out_specs=[pl.BlockSpec((None, blk, tc), lambda bi, c, t: (bi, t, c))] * len(out_dtypes),
        out_shape=[jax.ShapeDtypeStruct((bsz, nb * blk, ch), dt) for dt in out_dtypes],
        compiler_params=_cparams(3),
        name="dft_inverse",
    )(g_re, g_im, y_re, y_im, u_arr, gate_arr, bias)


def _ffn(h, w_gate, w_up, w_down, lead, rows, side=None, side_args=()):
    n_tiles = pl.cdiv(D_FF, MM_TN)
    outs = _matmul_wres([h], [w_gate, w_up], [(0, 0), (0, 1)], _ep_swiglu, D_FF, [BF16], tm=MM_TM, tn=MM_TN,
                        m_tiles=rows // MM_TM, n_tiles=n_tiles, k=D_MODEL, ch=MM_CHUNK, lead=[lead, lead],
                        last_width=D_FF - (n_tiles - 1) * MM_TN, side=side, side_args=side_args,
                        name="ffn_gate_up")
    f = _matmul_wres([outs[0]], [w_down], [(0, 0)], _ep_store, D_MODEL, [BRANCH_DTYPE], tm=DOWN_TM, tn=MM_TN,
                     m_tiles=rows // DOWN_TM, n_tiles=D_MODEL // MM_TN, k=D_FF, ch=DOWN_CHUNK, lead=[lead],
                     name="ffn_down")[0]
    return f, (outs[1] if side is not None else None)


def kernel(x, c, ctx, c_ctx, w_ada, b_ada, pre_g, post_g, ffn_w_gate, ffn_w_up, ffn_w_down, w_in, q_norm_g,
           k_norm_g, short_w, short_b, filt_w1, filt_b1, filt_w2, filt_b2, filt_w3, filt_b3, filt_freq,
           filt_w_out, hyena_bias, w_br_attn, w_br_hyena, w_out):
    bsz, n_lat, d = x.shape
    n_ctx = ctx.shape[1]
    rows_x, rows_c = bsz * n_lat, bsz * n_ctx
    x2d = x.reshape(rows_x, d)
    c2d = ctx.reshape(rows_c, d)
    l = 0

    cond = jnp.concatenate([c, c_ctx[None, :], jnp.zeros((3, d), F32)], axis=0)
    n_first = 2 * d
    bias = b_ada[l][None, :]
    mods_first = _ada(cond, w_ada, bias, l, n_first)
    pre = [pre_g[l, s][None, :] for s in range(3)]
    post = [post_g[l, s][None, :] for s in range(3)]

    h0 = _modulate_first(x2d, c2d, mods_first[:5].reshape(5, 2, d), pre[0], shift_i=0, scale_i=1)
    side = _SidePlan(lead=(l,), k=d, ts=ADA_SIDE_TILE, n_tiles=(N_MOD * d - n_first) // ADA_SIDE_TILE,
                     col0=n_first)
    f1, mods_rest = _ffn(h0, ffn_w_gate, ffn_w_up, ffn_w_down, (l, 0), rows_x + rows_c, side=side,
                         side_args=(cond, w_ada, bias[:, n_first:]))
    mods = jnp.concatenate([mods_first, mods_rest], axis=1)[:5].reshape(5, N_MOD, d)
    x1, h1 = _residual(f1, x2d, c2d, mods, post[0], pre[1], gate_i=2, weight=MACARON_W, shift_i=3, scale_i=4)

    tm, tn = MM_TM, MM_TN
    p = _matmul_wres([h1], [w_in], [(0, 0)], _ep_store, IN_COLS, [F32], tm=tm, tn=IN_TN, m_tiles=rows_x // tm,
                     n_tiles=IN_COLS // IN_TN, k=d, ch=IN_CHUNK, lead=[(l,)], name="in_proj")[0]
    pc_kv = _matmul_wres([h1], [w_in], [(0, 0)], _ep_store, 2 * KV_WIDTH, [F32], tm=CTX_TM, tn=tn,
                         m_tiles=rows_c // CTX_TM, n_tiles=2 * KV_WIDTH // tn, k=d, ch=MM_CHUNK, lead=[(l,)],
                         col0=Q_END, a_blk0=rows_x // CTX_TM, name="in_proj_ctx")[0]

    tables = _rope_tables(n_lat)
    q16, k16, v16 = _qkv_prep(p, q_norm_g[l][None, :], k_norm_g[l][None, :], tables)
    kc16, vc16 = _kv_prep_ctx(pc_kv, k_norm_g[l][None, :])
    k_all = jnp.concatenate([kc16.reshape(bsz, n_ctx, KV_WIDTH), k16.reshape(bsz, n_lat, KV_WIDTH)], axis=1)
    v_all = jnp.concatenate([vc16.reshape(bsz, n_ctx, KV_WIDTH), v16.reshape(bsz, n_lat, KV_WIDTH)], axis=1)
    attn_o = _attention(q16, k_all, v_all, n_lat)

    assert n_lat == 2 * CONV_BLOCK
    f_re, f_im, g_re, g_im = _dft_matrices(CONV_BLOCK)
    h_time = _hyena_filters(n_lat, filt_w1[l], filt_b1[l], filt_w2[l], filt_b2[l], filt_w3[l], filt_b3[l],
                            filt_freq[l], filt_w_out[l])
    spectra = _filter_spectra(h_time, f_re, f_im)
    p3 = p.reshape(bsz, n_lat, IN_COLS)
    u, v16h = _short_conv(p3, short_w[l], short_b[l][None, :])
    y_re, y_im = _dft_forward(v16h, f_re, f_im, spectra, order=0)
    z, z16 = _dft_inverse(y_re, y_im, g_re, g_im, u, 0, u, 1, hyena_bias[l, 0][None, :], [F32, BF16])
    y_re, y_im = _dft_forward(z16, f_re, f_im, spectra, order=1)
    hy_o = _dft_inverse(y_re, y_im, g_re, g_im, z, 0, u, 2, hyena_bias[l, 1][None, :], [BF16])[0]

    gate_spec = lambda off: pl.BlockSpec((tm, tn), lambda j, i: (i, off + j))
    merged = _matmul_wres([attn_o, hy_o.reshape(rows_x, HYENA_WIDTH)], [w_br_attn, w_br_hyena], [(0, 0), (1, 1)],
                          _ep_merge, d, [BF16], tm=tm, tn=tn, m_tiles=rows_x // tm, n_tiles=d // tn,
                          k=ATTN_WIDTH, ch=MM_CHUNK, lead=[(l,), (l,)], extras=(p, p),
                          extra_specs=(gate_spec(HY_END // tn), gate_spec((HY_END + d) // tn)), name="merge")[0]
    out = _matmul_wres([merged], [w_out], [(0, 0)], _ep_store, d, [BRANCH_DTYPE], tm=tm, tn=IN_TN,
                       m_tiles=rows_x // tm, n_tiles=d // IN_TN, k=d, ch=IN_CHUNK, lead=[(l,)], name="out_proj")[0]
    x2, h2 = _residual(out, x1, None, mods, post[1], pre[2], gate_i=5, weight=1.0, shift_i=6, scale_i=7)

    f2, _ = _ffn(h2, ffn_w_gate, ffn_w_up, ffn_w_down, (l, 1), rows_x)
    x3 = _residual(f2, x2, None, mods, post[2], None, gate_i=8, weight=MACARON_W)[0]
    return x3.reshape(bsz, n_lat, d)
```

```python
import functools
import math
from typing import NamedTuple

import jax
import jax.numpy as jnp
from jax import lax
from jax.experimental import pallas as pl
from jax.experimental.pallas import tpu as pltpu

F32 = jnp.float32
BF16 = jnp.bfloat16
BRANCH_DTYPE = BF16

D_MODEL = 4096
GRID_W = 64
HEAD_DIM = 128
N_Q_HEADS = 16
N_KV_HEADS = 4
Q_PER_KV = N_Q_HEADS // N_KV_HEADS
ATTN_WIDTH = N_Q_HEADS * HEAD_DIM
KV_WIDTH = N_KV_HEADS * HEAD_DIM
ROPE_THETA = 10000.0
AXIS_DIM = HEAD_DIM // 2
HYENA_WIDTH = D_MODEL // 2
FILTER_EMB = 33
FILTER_HIDDEN = 64
DECAY_TARGET = 1e-2
FAST_DECAY_PCT = 0.3
SLOW_DECAY_PCT = 1.5
D_FF = 11008
MACARON_W = 0.5
N_MOD = 9
NORM_EPS = 1e-6
Q_END = ATTN_WIDTH
K_END = Q_END + KV_WIDTH
V_END = K_END + KV_WIDTH
HY_END = V_END + 3 * HYENA_WIDTH
IN_COLS = HY_END + 2 * D_MODEL
SCORE_SCALE_LOG2 = math.log2(math.e) / math.sqrt(HEAD_DIM)

V7X_LANES = 128
BF16_SUBLANES = 16
V7X_VMEM_LIMIT_BYTES = 56 * 1024 * 1024

MM_TM = 1024
MM_TN = 512
MM_CHUNK = 1024
IN_TN = 1024
IN_CHUNK = 512
CTX_TM = 256
DOWN_TM = 512
DOWN_CHUNK = D_FF // 16
ROW_TILE = 256
ADA_SIDE_TILE = 256
CONV_BLOCK = 1024
DFT_ROWS = 64
FILT_PAD = V7X_LANES


def _cparams(n_axes):
    return pltpu.CompilerParams(dimension_semantics=("arbitrary",) * n_axes,
                                vmem_limit_bytes=V7X_VMEM_LIMIT_BYTES)


def _rms(x):
    return x * lax.rsqrt(jnp.mean(x * x, axis=-1, keepdims=True) + NORM_EPS)


def _sigmoid(x):
    return 0.5 * jnp.tanh(0.5 * x) + 0.5


def _ada_kernel(c_ref, w_ref, b_ref, o_ref):
    c = c_ref[...]
    s = (c * jax.nn.sigmoid(c)).astype(BF16)
    o_ref[...] = jnp.dot(s, w_ref[...].astype(BF16), preferred_element_type=F32) + b_ref[...]


def _ada(cond, w, b, layer, n):
    tn = 512
    return pl.pallas_call(
        _ada_kernel,
        grid=(n // tn,),
        in_specs=[pl.BlockSpec((8, D_MODEL), lambda j: (0, 0)),
                  pl.BlockSpec((None, D_MODEL, tn), lambda j: (layer, 0, j)),
                  pl.BlockSpec((1, tn), lambda j: (0, j))],
        out_specs=pl.BlockSpec((8, tn), lambda j: (0, j)),
        out_shape=jax.ShapeDtypeStruct((8, n), F32),
        compiler_params=_cparams(1),
        name="ada",
    )(cond, w, b)


def _mod_index(i, rows_per_batch_tiles):
    return jnp.minimum(i // rows_per_batch_tiles, 4)


def _modulate_kernel(x_ref, c_ref, m_ref, g_ref, o_ref, *, nx, shift_i, scale_i):
    i = pl.program_id(0)

    def body(src):
        y = _rms(src[...]) * g_ref[...]
        o_ref[...] = (y * (1.0 + m_ref[0, scale_i:scale_i + 1, :])
                      + m_ref[0, shift_i:shift_i + 1, :]).astype(o_ref.dtype)

    pl.when(i < nx)(lambda: body(x_ref))
    pl.when(i >= nx)(lambda: body(c_ref))


def _modulate_first(x2d, c2d, mods, g, shift_i, scale_i):
    tr = ROW_TILE
    nx, nc = x2d.shape[0] // tr, c2d.shape[0] // tr
    per_batch = (x2d.shape[0] // 4) // tr
    return pl.pallas_call(
        functools.partial(_modulate_kernel, nx=nx, shift_i=shift_i, scale_i=scale_i),
        grid=(nx + nc,),
        in_specs=[pl.BlockSpec((tr, D_MODEL), lambda i: (jnp.minimum(i, nx - 1), 0)),
                  pl.BlockSpec((tr, D_MODEL), lambda i: (jnp.maximum(i - nx, 0), 0)),
                  pl.BlockSpec((1, mods.shape[1], D_MODEL), lambda i: (_mod_index(i, per_batch), 0, 0)),
                  pl.BlockSpec((1, D_MODEL), lambda i: (0, 0))],
        out_specs=pl.BlockSpec((tr, D_MODEL), lambda i: (i, 0)),
        out_shape=jax.ShapeDtypeStruct(((nx + nc) * tr, D_MODEL), BF16),
        compiler_params=_cparams(1),
        name="modulate_first",
    )(x2d, c2d, mods, g)


def _resid_kernel(*refs, nx, two_source, gate_i, weight, shift_i, scale_i, with_next):
    it = iter(refs)
    f_ref = next(it)
    x_ref = next(it)
    c_ref = next(it) if two_source else None
    m_ref = next(it)
    pg_ref = next(it)
    ng_ref = next(it) if with_next else None
    xo_ref = next(it)
    ho_ref = next(it) if with_next else None
    i = pl.program_id(0)

    def body(src):
        y = _rms(f_ref[...].astype(F32)) * pg_ref[...]
        xn = src[...] + (weight * m_ref[0, gate_i:gate_i + 1, :]) * y
        xo_ref[...] = xn
        if with_next:
            h = _rms(xn) * ng_ref[...]
            ho_ref[...] = (h * (1.0 + m_ref[0, scale_i:scale_i + 1, :])
                           + m_ref[0, shift_i:shift_i + 1, :]).astype(ho_ref.dtype)

    if two_source:
        pl.when(i < nx)(lambda: body(x_ref))
        pl.when(i >= nx)(lambda: body(c_ref))
    else:
        body(x_ref)


def _residual(f, x2d, c2d, mods, post_g, next_g, *, gate_i, weight, shift_i=0, scale_i=0):
    tr = ROW_TILE
    rows = f.shape[0]
    n = rows // tr
    two_source = c2d is not None
    with_next = next_g is not None
    nx = x2d.shape[0] // tr if two_source else n
    per_batch = (8192 // 4) // tr
    row = pl.BlockSpec((tr, D_MODEL), lambda i: (i, 0))
    vec = pl.BlockSpec((1, D_MODEL), lambda i: (0, 0))
    in_specs = [row]
    args = [f]
    if two_source:
        in_specs += [pl.BlockSpec((tr, D_MODEL), lambda i: (jnp.minimum(i, nx - 1), 0)),
                     pl.BlockSpec((tr, D_MODEL), lambda i: (jnp.maximum(i - nx, 0), 0))]
        args += [x2d, c2d]
    else:
        in_specs += [row]
        args += [x2d]
    in_specs += [pl.BlockSpec((1, N_MOD, D_MODEL), lambda i: (_mod_index(i, per_batch), 0, 0)), vec]
    args += [mods, post_g]
    out_specs = [row]
    out_shape = [jax.ShapeDtypeStruct((rows, D_MODEL), F32)]
    if with_next:
        in_specs += [vec]
        args += [next_g]
        out_specs += [row]
        out_shape += [jax.ShapeDtypeStruct((rows, D_MODEL), BF16)]
    return pl.pallas_call(
        functools.partial(_resid_kernel, nx=nx, two_source=two_source, gate_i=gate_i, weight=weight,
                          shift_i=shift_i, scale_i=scale_i, with_next=with_next),
        grid=(n,),
        in_specs=in_specs,
        out_specs=out_specs,
        out_shape=out_shape,
        compiler_params=_cparams(1),
        name="residual",
    )(*args)


class _WeightPlan(NamedTuple):
    k: int
    ch: int
    tn: int
    n_tiles: int
    m_tiles: int
    col0: int
    last_width: int
    lead: tuple


def _wres_kernel(*refs, n_a, n_w, pairs, n_extra, n_out, epilogue, plan, side):
    it = iter(refs)
    a_refs = [next(it) for _ in range(n_a)]
    w_refs = [next(it) for _ in range(n_w)]
    e_refs = [next(it) for _ in range(n_extra)]
    if side is not None:
        side_c_ref, side_w_ref, side_b_ref = next(it), next(it), next(it)
    o_refs = [next(it) for _ in range(n_out)]
    if side is not None:
        side_o_ref = next(it)
    wbf, stage, sem = next(it), next(it), next(it)
    if side is not None:
        side_stage, side_sem = next(it), next(it)
    j = pl.program_id(0)
    i = pl.program_id(1)
    slot = j % 2
    per_w = plan.k // plan.ch
    n_chunks = n_w * per_w
    ragged = plan.last_width != plan.tn

    def chunk_copy(tile, c, width):
        w, r = divmod(c, per_w)
        col = plan.col0 + tile * plan.tn
        if not isinstance(col, int):
            col = pl.multiple_of(col, V7X_LANES)
        src = w_refs[w].at[(*plan.lead[w], pl.ds(r * plan.ch, plan.ch), pl.ds(col, width))]
        return pltpu.make_async_copy(src, stage.at[c % 2, :, pl.ds(0, width)], sem.at[c % 2])

    def width_variants(tile):
        if not ragged:
            return [(plan.tn, tile < plan.n_tiles)]
        return [(plan.tn, tile < plan.n_tiles - 1), (plan.last_width, tile == plan.n_tiles - 1)]

    @pl.when(jnp.logical_and(j == 0, i == 0))
    def _():
        chunk_copy(0, 0, plan.tn).start()
        for c in range(n_chunks):
            w, r = divmod(c, per_w)
            chunk_copy(0, c, plan.tn).wait()
            if c + 1 < n_chunks:
                chunk_copy(0, c + 1, plan.tn).start()
            wbf[0, w, pl.ds(r * plan.ch, plan.ch), :] = stage[c % 2].astype(BF16)
        second = plan.last_width if plan.n_tiles == 2 else plan.tn
        chunk_copy(1, 0, second).start()

    for width, cond in width_variants(j + 1):
        for c in range(n_chunks):
            @pl.when(jnp.logical_and(cond, i == c))
            def _(width=width, c=c):
                chunk_copy(j + 1, c, width).wait()
                if c + 1 < n_chunks:
                    chunk_copy(j + 1, c + 1, width).start()

    for width, cond in width_variants(j + 2):
        @pl.when(jnp.logical_and(cond, i == plan.m_tiles - 1))
        def _(width=width):
            chunk_copy(j + 2, 0, width).start()

    if side is not None:
        _side_stream_dma(side, side_w_ref, side_stage, side_sem, j * plan.m_tiles + i)

    def compute(width):
        c_now = jnp.minimum(i, n_chunks - 1)
        w_now = c_now // per_w
        row_now = pl.multiple_of((c_now % per_w) * plan.ch, BF16_SUBLANES)
        wbf[1 - slot, w_now, pl.ds(row_now, plan.ch), :] = stage[c_now % 2].astype(BF16)
        if side is not None:
            _side_stream_compute(side, side_c_ref, side_b_ref, side_o_ref, side_stage, j * plan.m_tiles + i)
        prods = [jnp.dot(a_refs[ai][...], wbf[slot, wi, :, pl.ds(0, width)], preferred_element_type=F32)
                 for ai, wi in pairs]
        epilogue(prods, e_refs, o_refs, width)

    if ragged:
        pl.when(j < plan.n_tiles - 1)(lambda: compute(plan.tn))
        pl.when(j == plan.n_tiles - 1)(lambda: compute(plan.last_width))
    else:
        compute(plan.tn)


class _SidePlan(NamedTuple):
    lead: tuple
    k: int
    ts: int
    n_tiles: int
    col0: int


def _side_stream_dma(side, w_ref, stage, sem, g):
    def copy(tile, slot):
        col = pl.multiple_of(side.col0 + tile * side.ts, V7X_LANES)
        return pltpu.make_async_copy(w_ref.at[(*side.lead, slice(None), pl.ds(col, side.ts))],
                                     stage.at[slot], sem.at[slot])

    pl.when(g == 0)(lambda: copy(g, 0).start())
    for slot in range(2):
        pl.when(jnp.logical_and(g < side.n_tiles, g % 2 == slot))(lambda slot=slot: copy(g, slot).wait())
    for slot in range(2):
        pl.when(jnp.logical_and(g + 1 < side.n_tiles, (g + 1) % 2 == slot))(
            lambda slot=slot: copy(g + 1, slot).start())


def _side_stream_compute(side, c_ref, b_ref, o_ref, stage, g):
    t = jnp.minimum(g, side.n_tiles - 1)
    c = c_ref[...]
    s = (c * jax.nn.sigmoid(c)).astype(BF16)
    o_ref[...] = jnp.dot(s, stage[t % 2].astype(BF16), preferred_element_type=F32) + b_ref[...]


def _matmul_wres(a_list, w_list, pairs, epilogue, out_cols, out_dtypes, *, tm, tn, m_tiles, n_tiles, k, ch,
                 lead, col0=0, last_width=None, a_blk0=0, extras=(), extra_specs=(), side=None, side_args=(),
                 name="matmul"):
    plan = _WeightPlan(k=k, ch=ch, tn=tn, n_tiles=n_tiles, m_tiles=m_tiles, col0=col0,
                       last_width=last_width or tn, lead=tuple(lead))
    n_chunks = len(w_list) * (k // ch)
    assert n_chunks <= m_tiles and n_chunks % 2 == 0 and n_tiles >= 2 and k % ch == 0
    in_specs = [pl.BlockSpec((tm, k), lambda j, i: (i + a_blk0, 0)) for _ in a_list]
    in_specs += [pl.BlockSpec(memory_space=pl.ANY) for _ in w_list]
    in_specs += list(extra_specs)
    out_specs = [pl.BlockSpec((tm, tn), lambda j, i: (i, j)) for _ in out_dtypes]
    out_shape = [jax.ShapeDtypeStruct((m_tiles * tm, out_cols), dt) for dt in out_dtypes]
    scratch = [pltpu.VMEM((2, len(w_list), k, tn), BF16), pltpu.VMEM((2, ch, tn), F32),
               pltpu.SemaphoreType.DMA((2,))]
    if side is not None:
        assert side.n_tiles <= n_tiles * m_tiles
        side_tile = lambda j, i: (0, jnp.minimum(j * m_tiles + i, side.n_tiles - 1))
        in_specs += [pl.BlockSpec((8, side.k), lambda j, i: (0, 0)), pl.BlockSpec(memory_space=pl.ANY),
                     pl.BlockSpec((1, side.ts), side_tile)]
        out_specs += [pl.BlockSpec((8, side.ts), side_tile)]
        out_shape += [jax.ShapeDtypeStruct((8, side.n_tiles * side.ts), F32)]
        scratch += [pltpu.VMEM((2, side.k, side.ts), F32), pltpu.SemaphoreType.DMA((2,))]
    return pl.pallas_call(
        functools.partial(_wres_kernel, n_a=len(a_list), n_w=len(w_list), pairs=tuple(pairs),
                          n_extra=len(extras), n_out=len(out_dtypes), epilogue=epilogue, plan=plan, side=side),
        grid=(n_tiles, m_tiles),
        in_specs=in_specs,
        out_specs=out_specs,
        out_shape=out_shape,
        scratch_shapes=scratch,
        compiler_params=_cparams(2),
        name=name,
    )(*a_list, *w_list, *extras, *side_args)


def _ep_store(accs, e_refs, o_refs, width):
    o_refs[0][:, pl.ds(0, width)] = accs[0].astype(o_refs[0].dtype)


def _ep_swiglu(accs, e_refs, o_refs, width):
    g, u = accs
    o_refs[0][:, pl.ds(0, width)] = (g * _sigmoid(g) * u).astype(o_refs[0].dtype)


def _ep_merge(accs, e_refs, o_refs, width):
    a, h = accs
    ga, gh = e_refs
    o_refs[0][...] = (_sigmoid(ga[...]) * a + _sigmoid(gh[...]) * h).astype(o_refs[0].dtype)


def _rope_tables(n_lat):
    t = jnp.arange(n_lat, dtype=jnp.int32)
    row = (t // GRID_W).astype(F32)
    col = (t % GRID_W).astype(F32)
    inv_freq = ROPE_THETA ** (-jnp.arange(0, AXIS_DIM, 2, dtype=F32) / AXIS_DIM)
    ang = jnp.concatenate([jnp.tile(row[:, None] * inv_freq, (1, 2)),
                           jnp.tile(col[:, None] * inv_freq, (1, 2))], axis=-1)
    lane = jnp.arange(HEAD_DIM, dtype=jnp.int32)
    first = (lane % AXIS_DIM) < (AXIS_DIM // 2)
    cos = jnp.cos(ang)
    sin = jnp.sin(ang)
    sin_a = jnp.where(first, -sin, 0.0)
    sin_b = jnp.where(first, 0.0, sin)
    return cos, sin_a, sin_b


def _head_prep(x, g, cos, sin_a, sin_b, rope):
    y = _rms(x) * g
    if rope:
        up = pltpu.roll(y, HEAD_DIM - AXIS_DIM // 2, axis=1)
        dn = pltpu.roll(y, AXIS_DIM // 2, axis=1)
        y = y * cos + up * sin_a + dn * sin_b
    return y


def _qkv_prep_kernel(q_ref, kv_ref, qg_ref, kg_ref, cos_ref, sa_ref, sb_ref, qo_ref, ko_ref, vo_ref):
    cos, sa, sb = cos_ref[...], sa_ref[...], sb_ref[...]
    for h in range(N_Q_HEADS):
        sl = slice(h * HEAD_DIM, (h + 1) * HEAD_DIM)
        q = _head_prep(q_ref[:, sl], qg_ref[...], cos, sa, sb, True)
        qo_ref[:, sl] = (q * SCORE_SCALE_LOG2).astype(qo_ref.dtype)
    for h in range(N_KV_HEADS):
        sl = slice(h * HEAD_DIM, (h + 1) * HEAD_DIM)
        ko_ref[:, sl] = _head_prep(kv_ref[:, sl], kg_ref[...], cos, sa, sb, True).astype(ko_ref.dtype)
    vo_ref[...] = kv_ref[:, KV_WIDTH:].astype(vo_ref.dtype)


def _qkv_prep(p, q_g, k_g, tables):
    tr = ROW_TILE
    rows = p.shape[0]
    cos, sin_a, sin_b = tables
    nt = cos.shape[0] // tr
    tab = pl.BlockSpec((tr, HEAD_DIM), lambda i: (i % nt, 0))
    vec = pl.BlockSpec((1, HEAD_DIM), lambda i: (0, 0))
    kv_out = pl.BlockSpec((tr, KV_WIDTH), lambda i: (i, 0))
    return pl.pallas_call(
        _qkv_prep_kernel,
        grid=(rows // tr,),
        in_specs=[pl.BlockSpec((tr, ATTN_WIDTH), lambda i: (i, 0)),
                  pl.BlockSpec((tr, 2 * KV_WIDTH), lambda i: (i, Q_END // (2 * KV_WIDTH))),
                  vec, vec, tab, tab, tab],
        out_specs=[pl.BlockSpec((tr, ATTN_WIDTH), lambda i: (i, 0)), kv_out, kv_out],
        out_shape=[jax.ShapeDtypeStruct((rows, ATTN_WIDTH), BF16),
                   jax.ShapeDtypeStruct((rows, KV_WIDTH), BF16),
                   jax.ShapeDtypeStruct((rows, KV_WIDTH), BF16)],
        compiler_params=_cparams(1),
        name="qkv_prep",
    )(p, p, q_g, k_g, cos, sin_a, sin_b)


def _kv_prep_ctx_kernel(kv_ref, kg_ref, ko_ref, vo_ref):
    for h in range(N_KV_HEADS):
        sl = slice(h * HEAD_DIM, (h + 1) * HEAD_DIM)
        ko_ref[:, sl] = _head_prep(kv_ref[:, sl], kg_ref[...], None, None, None, False).astype(ko_ref.dtype)
    vo_ref[...] = kv_ref[:, KV_WIDTH:].astype(vo_ref.dtype)


def _kv_prep_ctx(pc_kv, k_g):
    tr = ROW_TILE
    rows = pc_kv.shape[0]
    kv_out = pl.BlockSpec((tr, KV_WIDTH), lambda i: (i, 0))
    return pl.pallas_call(
        _kv_prep_ctx_kernel,
        grid=(rows // tr,),
        in_specs=[pl.BlockSpec((tr, 2 * KV_WIDTH), lambda i: (i, 0)),
                  pl.BlockSpec((1, HEAD_DIM), lambda i: (0, 0))],
        out_specs=[kv_out, kv_out],
        out_shape=[jax.ShapeDtypeStruct((rows, KV_WIDTH), BF16)] * 2,
        compiler_params=_cparams(1),
        name="kv_prep_ctx",
    )(pc_kv, k_g)


def _attn_kernel(q_ref, k_ref, v_ref, o_ref):
    k = k_ref[...]
    v = v_ref[...]
    for g in range(Q_PER_KV):
        sl = slice(g * HEAD_DIM, (g + 1) * HEAD_DIM)
        s = lax.dot_general(q_ref[:, sl], k, (((1,), (1,)), ((), ())), preferred_element_type=F32)
        m = jnp.max(s, axis=-1, keepdims=True)
        p = jnp.exp2(s - m)
        l = jnp.sum(p, axis=-1, keepdims=True)
        o = jnp.dot(p.astype(BF16), v, preferred_element_type=F32)
        o_ref[:, sl] = (o / l).astype(o_ref.dtype)


def _attention(q, k_all, v_all, n_lat):
    tq = 512
    b, t, _ = k_all.shape
    nq = n_lat // tq
    gw = Q_PER_KV * HEAD_DIM
    return pl.pallas_call(
        _attn_kernel,
        grid=(b, N_KV_HEADS, nq),
        in_specs=[pl.BlockSpec((tq, gw), lambda bi, h, i: (bi * nq + i, h)),
                  pl.BlockSpec((None, t, HEAD_DIM), lambda bi, h, i: (bi, 0, h)),
                  pl.BlockSpec((None, t, HEAD_DIM), lambda bi, h, i: (bi, 0, h))],
        out_specs=pl.BlockSpec((tq, gw), lambda bi, h, i: (bi * nq + i, h)),
        out_shape=jax.ShapeDtypeStruct(q.shape, BF16),
        compiler_params=_cparams(3),
        name="attention",
    )(q, k_all, v_all)


def _filter_kernel(z_ref, w1_ref, b1_ref, w2_ref, b2_ref, w3_ref, b3_ref, fr_ref, wo_ref, t_ref, ad_ref, o_ref,
                   a_ref):
    j = pl.program_id(0)
    hi = lax.Precision.HIGHEST

    @pl.when(j == 0)
    def _():
        fr = fr_ref[...]
        a = jnp.sin(fr * (jnp.dot(z_ref[...], w1_ref[...], precision=hi, preferred_element_type=F32)
                          + b1_ref[...]))
        a = jnp.sin(fr * (jnp.dot(a, w2_ref[...], precision=hi, preferred_element_type=F32) + b2_ref[...]))
        a_ref[...] = jnp.sin(fr * (jnp.dot(a, w3_ref[...], precision=hi, preferred_element_type=F32)
                                   + b3_ref[...]))

    h = jnp.dot(a_ref[...].astype(BF16), wo_ref[...].astype(BF16), preferred_element_type=F32)
    h = h * jnp.exp(-t_ref[...] * ad_ref[...])
    backward = (j // 2) % 2 == 1
    row = lax.broadcasted_iota(jnp.int32, h.shape, 0)
    o_ref[...] = jnp.where(jnp.logical_and(backward, row == 0), 0.0, h)


def _pad2(a, rows, cols):
    return jnp.pad(a, ((0, rows - a.shape[0]), (0, cols - a.shape[1])))


def _hyena_filters(n, w1, b1, w2, b2, w3, b3, freq, w_out):
    t = jnp.linspace(0.0, 1.0, n, dtype=F32)[:, None]
    bands = (FILTER_EMB - 1) // 2
    w = 2.0 * math.pi * jnp.arange(n, dtype=F32)[:, None] / n
    f = jnp.linspace(1e-4, bands - 1, bands, dtype=F32)[None, :]
    z = jnp.concatenate([t, jnp.cos(f * w), -jnp.sin(f * w)], axis=-1)
    max_decay = math.log(DECAY_TARGET) / FAST_DECAY_PCT
    min_decay = math.log(DECAY_TARGET) / SLOW_DECAY_PCT
    ad = jnp.abs(jnp.linspace(min_decay, max_decay, HYENA_WIDTH, dtype=F32))[None, :]
    p = FILT_PAD
    tn = 1024
    n_out = w_out.shape[1]
    full = lambda shape: pl.BlockSpec(shape, lambda j: (0, 0))
    return pl.pallas_call(
        _filter_kernel,
        grid=(n_out // tn,),
        in_specs=[full((n, p)), full((p, p)), full((1, p)), full((p, p)), full((1, p)), full((p, p)),
                  full((1, p)), full((1, p)),
                  pl.BlockSpec((p, tn), lambda j: (0, j)),
                  full((n, 1)),
                  pl.BlockSpec((1, tn), lambda j: (0, j % (HYENA_WIDTH // tn)))],
        out_specs=pl.BlockSpec((n, tn), lambda j: (0, j)),
        out_shape=jax.ShapeDtypeStruct((n, n_out), F32),
        scratch_shapes=[pltpu.VMEM((n, p), F32)],
        compiler_params=_cparams(1),
        name="hyena_filters",
    )(_pad2(z, n, p), _pad2(w1, p, p), _pad2(b1[None, :], 1, p), _pad2(w2, p, p), _pad2(b2[None, :], 1, p),
      _pad2(w3, p, p), _pad2(b3[None, :], 1, p), _pad2(freq[None, :], 1, p), _pad2(w_out, p, n_out), t, ad)


def _dft_matrix_kernel(ca_ref, sa_ref, cb_ref, sb_ref, alt_ref, cs_ref, fre_ref, fim_ref, gre_ref, gim_ref):
    i = pl.program_id(0)
    ca, sa = ca_ref[...], sa_ref[...]
    cb, sb = cb_ref[...], sb_ref[...]
    cos = ca * cb - sa * sb
    sin = sa * cb + ca * sb
    cs = cs_ref[...]
    row = lax.broadcasted_iota(jnp.int32, cos.shape, 0)
    col = lax.broadcasted_iota(jnp.int32, cos.shape, 1)
    nyquist_row = jnp.logical_and(i == 0, row == 0)
    alt_rows = jnp.where(row % 2 == 0, 1.0, -1.0)
    fre_ref[...] = cos.astype(fre_ref.dtype)
    fim_ref[...] = jnp.where(nyquist_row, alt_ref[...], -sin).astype(fim_ref.dtype)
    gre_ref[...] = (cos * cs).astype(gre_ref.dtype)
    gim_ref[...] = (jnp.where(col == 0, alt_rows, -sin) * cs).astype(gim_ref.dtype)


def _dft_matrices(n):
    big = 2 * n
    rb = DFT_ROWS
    t = jnp.arange(n, dtype=jnp.int32)[None, :]
    unit = 2.0 * math.pi / big
    coarse = ((rb * jnp.arange(n // rb, dtype=jnp.int32)[:, None] * t) % big).astype(F32) * unit
    fine = ((jnp.arange(rb, dtype=jnp.int32)[:, None] * t) % big).astype(F32) * unit
    alt = jnp.where(t % 2 == 0, 1.0, -1.0).astype(F32)
    cs = jnp.where(t == 0, 1.0 / big, 2.0 / big).astype(F32)
    coarse_spec = pl.BlockSpec((None, 1, n), lambda i: (i, 0, 0))
    whole = lambda r: pl.BlockSpec((r, n), lambda i: (0, 0))
    out_spec = pl.BlockSpec((rb, n), lambda i: (i, 0))
    return pl.pallas_call(
        _dft_matrix_kernel,
        grid=(n // rb,),
        in_specs=[coarse_spec, coarse_spec, whole(rb), whole(rb), whole(1), whole(1)],
        out_specs=[out_spec] * 4,
        out_shape=[jax.ShapeDtypeStruct((n, n), BF16)] * 4,
        compiler_params=_cparams(1),
        name="dft_matrices",
    )(jnp.cos(coarse)[:, None, :], jnp.sin(coarse)[:, None, :], jnp.cos(fine), jnp.sin(fine), alt, cs)


def _spectrum_kernel(fre_ref, fim_ref, flo_ref, fhi_ref, blo_ref, bhi_ref, k0r_ref, k0i_ref, k1r_ref, k1i_ref,
                     kmr_ref, kmi_ref):
    fre = fre_ref[...]
    fim = fim_ref[...]

    def spectrum(x_ref):
        x = x_ref[...].astype(BF16)
        return jnp.dot(fre, x, preferred_element_type=F32), jnp.dot(fim, x, preferred_element_type=F32)

    alr, ali = spectrum(flo_ref)
    ahr, ahi = spectrum(fhi_ref)
    blr, bli = spectrum(blo_ref)
    bhr, bhi = spectrum(bhi_ref)
    h0 = flo_ref[0:1, :].astype(BF16).astype(F32)
    row = lax.broadcasted_iota(jnp.int32, alr.shape, 0)
    first = row == 0
    sgn = jnp.where(row % 2 == 0, 1.0, -1.0)
    k0r_ref[...] = alr + blr
    k0i_ref[...] = jnp.where(first, ali + bli, ali - bli)
    k1r_ref[...] = ahr + sgn * (alr - h0)
    k1i_ref[...] = jnp.where(first, ahi + ali - h0, ahi + sgn * ali)
    kmr_ref[...] = bhr + sgn * blr
    kmi_ref[...] = jnp.where(first, bhi + bli, -bhi - sgn * bli)


def _filter_spectra(h_time, f_re, f_im):
    blk = CONV_BLOCK
    tc = 256
    per = HYENA_WIDTH // tc
    whole = pl.BlockSpec((blk, blk), lambda j: (0, 0))
    taps = lambda half, direction: pl.BlockSpec(
        (blk, tc), lambda j: (half, (j // per) * 2 * per + direction * per + j % per))
    return pl.pallas_call(
        _spectrum_kernel,
        grid=(2 * per,),
        in_specs=[whole, whole, taps(0, 0), taps(1, 0), taps(0, 1), taps(1, 1)],
        out_specs=[pl.BlockSpec((blk, tc), lambda j: (0, j))] * 6,
        out_shape=[jax.ShapeDtypeStruct((blk, 2 * HYENA_WIDTH), F32)] * 6,
        compiler_params=_cparams(1),
        name="filter_spectra",
    )(f_re, f_im, h_time, h_time, h_time, h_time)


def _short_conv_kernel(p_ref, w_ref, b_ref, u_ref, v16_ref):
    j = pl.program_id(1)
    x = p_ref[...]
    n = x.shape[0]
    row = lax.broadcasted_iota(jnp.int32, x.shape, 0)
    prev = jnp.where(row == 0, 0.0, pltpu.roll(x, 1, axis=0))
    nxt = jnp.where(row == n - 1, 0.0, pltpu.roll(x, n - 1, axis=0))
    u = prev * w_ref[0:1, :] + x * w_ref[1:2, :] + nxt * w_ref[2:3, :] + b_ref[...]
    u_ref[...] = u

    @pl.when(j < HYENA_WIDTH // u.shape[1])
    def _():
        v16_ref[...] = u.astype(v16_ref.dtype)


def _short_conv(p3, w, b):
    bsz, n, _ = p3.shape
    tc = 512
    nj = 3 * HYENA_WIDTH // tc
    nv = HYENA_WIDTH // tc
    off = V_END // tc
    return pl.pallas_call(
        _short_conv_kernel,
        grid=(bsz, nj),
        in_specs=[pl.BlockSpec((None, n, tc), lambda bi, j: (bi, 0, off + j)),
                  pl.BlockSpec((3, tc), lambda bi, j: (0, j)),
                  pl.BlockSpec((1, tc), lambda bi, j: (0, j))],
        out_specs=[pl.BlockSpec((None, n, tc), lambda bi, j: (bi, 0, j)),
                   pl.BlockSpec((None, n, tc), lambda bi, j: (bi, 0, jnp.minimum(j, nv - 1)))],
        out_shape=[jax.ShapeDtypeStruct((bsz, n, 3 * HYENA_WIDTH), F32),
                   jax.ShapeDtypeStruct((bsz, n, HYENA_WIDTH), BF16)],
        compiler_params=_cparams(2),
        name="short_conv",
    )(p3, w, b)


def _packed_cmul(xr, xi, kr, ki, first):
    return (jnp.where(first, xr * kr, xr * kr - xi * ki), jnp.where(first, xi * ki, xr * ki + xi * kr))


def _dft_fwd_kernel(fre_ref, fim_ref, u_ref, k0r_ref, k0i_ref, k1r_ref, k1i_ref, kmr_ref, kmi_ref, yre_ref,
                    yim_ref):
    fre = fre_ref[...]
    fim = fim_ref[...]
    blk = fre.shape[0]
    u0 = u_ref[pl.ds(0, blk), :]
    u1 = u_ref[pl.ds(blk, blk), :]
    x0r = jnp.dot(fre, u0, preferred_element_type=F32)
    x0i = jnp.dot(fim, u0, preferred_element_type=F32)
    x1r = jnp.dot(fre, u1, preferred_element_type=F32)
    x1i = jnp.dot(fim, u1, preferred_element_type=F32)
    first = lax.broadcasted_iota(jnp.int32, x0r.shape, 0) == 0
    k0r, k0i = k0r_ref[...], k0i_ref[...]
    ar, ai = _packed_cmul(x0r, x0i, k0r, k0i, first)
    br, bi = _packed_cmul(x1r, x1i, kmr_ref[...], kmi_ref[...], first)
    yre_ref[0] = (ar + br).astype(yre_ref.dtype)
    yim_ref[0] = (ai + bi).astype(yim_ref.dtype)
    cr, ci = _packed_cmul(x0r, x0i, k1r_ref[...], k1i_ref[...], first)
    dr, di = _packed_cmul(x1r, x1i, k0r, k0i, first)
    yre_ref[1] = (cr + dr).astype(yre_ref.dtype)
    yim_ref[1] = (ci + di).astype(yim_ref.dtype)


def _dft_forward(u16, f_re, f_im, spectra, order):
    bsz, n, ch = u16.shape
    blk = CONV_BLOCK
    tc = 256
    nc = ch // tc
    whole = pl.BlockSpec((blk, blk), lambda c, bi: (0, 0))
    seg = pl.BlockSpec((blk, tc), lambda c, bi: (0, order * nc + c))
    return pl.pallas_call(
        _dft_fwd_kernel,
        grid=(nc, bsz),
        in_specs=[whole, whole, pl.BlockSpec((None, n, tc), lambda c, bi: (bi, 0, c))] + [seg] * 6,
        out_specs=[pl.BlockSpec((None, 2, blk, tc), lambda c, bi: (bi, 0, 0, c))] * 2,
        out_shape=[jax.ShapeDtypeStruct((bsz, 2, blk, ch), BF16)] * 2,
        compiler_params=_cparams(2),
        name="dft_forward",
    )(f_re, f_im, u16, *spectra)


def _dft_inv_kernel(gre_ref, gim_ref, yre_ref, yim_ref, u_ref, gate_ref, bias_ref, *o_refs):
    y = (jnp.dot(gre_ref[...], yre_ref[...], preferred_element_type=F32)
         + jnp.dot(gim_ref[...], yim_ref[...], preferred_element_type=F32))
    z = gate_ref[...] * (y + u_ref[...] * bias_ref[...])
    for o in o_refs:
        o[...] = z.astype(o.dtype)


def _dft_inverse(y_re, y_im, g_re, g_im, u_arr, u_blk0, gate_arr, gate_blk0, bias, out_dtypes):
    bsz, nb, blk, ch = y_re.shape
    tc = 512
    nc = ch // tc
    whole = pl.BlockSpec((blk, blk), lambda bi, c, t: (0, 0))
    spec = pl.BlockSpec((None, None, blk, tc), lambda bi, c, t: (bi, t, 0, c))
    return pl.pallas_call(
        _dft_inv_kernel,
        grid=(bsz, nc, nb),
        in_specs=[whole, whole, spec, spec,
                  pl.BlockSpec((None, blk, tc), lambda bi, c, t: (bi, t, u_blk0 * nc + c)),
                  pl.BlockSpec((None, blk, tc), lambda bi, c, t: (bi, t, gate_blk0 * nc + c)),
                  pl.BlockSpec((1, tc), lambda bi, c, t: (0, c))],
        out_specs=[pl.BlockSpec((None, blk, tc), lambda bi, c, t: (bi, t, c))] * len(out_dtypes),
        out_shape=[jax.ShapeDtypeStruct((bsz, nb * blk, ch), dt) for dt in out_dtypes],
        compiler_params=_cparams(3),
        name="dft_inverse",
    )(g_re, g_im, y_re, y_im, u_arr, gate_arr, bias)


def _ffn(h, w_gate, w_up, w_down, lead, rows, side=None, side_args=()):
    n_tiles = pl.cdiv(D_FF, MM_TN)
    outs = _matmul_wres([h], [w_gate, w_up], [(0, 0), (0, 1)], _ep_swiglu, D_FF, [BF16], tm=MM_TM, tn=MM_TN,
                        m_tiles=rows // MM_TM, n_tiles=n_tiles, k=D_MODEL, ch=MM_CHUNK, lead=[lead, lead],
                        last_width=D_FF - (n_tiles - 1) * MM_TN, side=side, side_args=side_args,
                        name="ffn_gate_up")
    f = _matmul_wres([outs[0]], [w_down], [(0, 0)], _ep_store, D_MODEL, [BRANCH_DTYPE], tm=DOWN_TM, tn=MM_TN,
                     m_tiles=rows // DOWN_TM, n_tiles=D_MODEL // MM_TN, k=D_FF, ch=DOWN_CHUNK, lead=[lead],
                     name="ffn_down")[0]
    return f, (outs[1] if side is not None else None)


def kernel(x, c, ctx, c_ctx, w_ada, b_ada, pre_g, post_g, ffn_w_gate, ffn_w_up, ffn_w_down, w_in, q_norm_g,
           k_norm_g, short_w, short_b, filt_w1, filt_b1, filt_w2, filt_b2, filt_w3, filt_b3, filt_freq,
           filt_w_out, hyena_bias, w_br_attn, w_br_hyena, w_out):
    bsz, n_lat, d = x.shape
    n_ctx = ctx.shape[1]
    rows_x, rows_c = bsz * n_lat, bsz * n_ctx
    x2d = x.reshape(rows_x, d)
    c2d = ctx.reshape(rows_c, d)
    l = 0

    cond = jnp.concatenate([c, c_ctx[None, :], jnp.zeros((3, d), F32)], axis=0)
    n_first = 2 * d
    bias = b_ada[l][None, :]
    mods_first = _ada(cond, w_ada, bias, l, n_first)
    pre = [pre_g[l, s][None, :] for s in range(3)]
    post = [post_g[l, s][None, :] for s in range(3)]

    h0 = _modulate_first(x2d, c2d, mods_first[:5].reshape(5, 2, d), pre[0], shift_i=0, scale_i=1)
    side = _SidePlan(lead=(l,), k=d, ts=ADA_SIDE_TILE, n_tiles=(N_MOD * d - n_first) // ADA_SIDE_TILE,
                     col0=n_first)
    f1, mods_rest = _ffn(h0, ffn_w_gate, ffn_w_up, ffn_w_down, (l, 0), rows_x + rows_c, side=side,
                         side_args=(cond, w_ada, bias[:, n_first:]))
    mods = jnp.concatenate([mods_first, mods_rest], axis=1)[:5].reshape(5, N_MOD, d)
    x1, h1 = _residual(f1, x2d, c2d, mods, post[0], pre[1], gate_i=2, weight=MACARON_W, shift_i=3, scale_i=4)

    tm, tn = MM_TM, MM_TN
    p = _matmul_wres([h1], [w_in], [(0, 0)], _ep_store, IN_COLS, [F32], tm=tm, tn=IN_TN, m_tiles=rows_x // tm,
                     n_tiles=IN_COLS // IN_TN, k=d, ch=IN_CHUNK, lead=[(l,)], name="in_proj")[0]
    pc_kv = _matmul_wres([h1], [w_in], [(0, 0)], _ep_store, 2 * KV_WIDTH, [F32], tm=CTX_TM, tn=tn,
                         m_tiles=rows_c // CTX_TM, n_tiles=2 * KV_WIDTH // tn, k=d, ch=MM_CHUNK, lead=[(l,)],
                         col0=Q_END, a_blk0=rows_x // CTX_TM, name="in_proj_ctx")[0]

    tables = _rope_tables(n_lat)
    q16, k16, v16 = _qkv_prep(p, q_norm_g[l][None, :], k_norm_g[l][None, :], tables)
    kc16, vc16 = _kv_prep_ctx(pc_kv, k_norm_g[l][None, :])
    k_all = jnp.concatenate([kc16.reshape(bsz, n_ctx, KV_WIDTH), k16.reshape(bsz, n_lat, KV_WIDTH)], axis=1)
    v_all = jnp.concatenate([vc16.reshape(bsz, n_ctx, KV_WIDTH), v16.reshape(bsz, n_lat, KV_WIDTH)], axis=1)
    attn_o = _attention(q16, k_all, v_all, n_lat)

    assert n_lat == 2 * CONV_BLOCK
    f_re, f_im, g_re, g_im = _dft_matrices(CONV_BLOCK)
    h_time = _hyena_filters(n_lat, filt_w1[l], filt_b1[l], filt_w2[l], filt_b2[l], filt_w3[l], filt_b3[l],
                            filt_freq[l], filt_w_out[l])
    spectra = _filter_spectra(h_time, f_re, f_im)
    p3 = p.reshape(bsz, n_lat, IN_COLS)
    u, v16h = _short_conv(p3, short_w[l], short_b[l][None, :])
    y_re, y_im = _dft_forward(v16h, f_re, f_im, spectra, order=0)
    z, z16 = _dft_inverse(y_re, y_im, g_re, g_im, u, 0, u, 1, hyena_bias[l, 0][None, :], [F32, BF16])
    y_re, y_im = _dft_forward(z16, f_re, f_im, spectra, order=1)
    hy_o = _dft_inverse(y_re, y_im, g_re, g_im, z, 0, u, 2, hyena_bias[l, 1][None, :], [BF16])[0]

    gate_spec = lambda off: pl.BlockSpec((tm, tn), lambda j, i: (i, off + j))
    merged = _matmul_wres([attn_o, hy_o.reshape(rows_x, HYENA_WIDTH)], [w_br_attn, w_br_hyena], [(0, 0), (1, 1)],
                          _ep_merge, d, [BF16], tm=tm, tn=tn, m_tiles=rows_x // tm, n_tiles=d // tn,
                          k=ATTN_WIDTH, ch=MM_CHUNK, lead=[(l,), (l,)], extras=(p, p),
                          extra_specs=(gate_spec(HY_END // tn), gate_spec((HY_END + d) // tn)), name="merge")[0]
    out = _matmul_wres([merged], [w_out], [(0, 0)], _ep_store, d, [BRANCH_DTYPE], tm=tm, tn=IN_TN,
                       m_tiles=rows_x // tm, n_tiles=d // IN_TN, k=d, ch=IN_CHUNK, lead=[(l,)], name="out_proj")[0]
    x2, h2 = _residual(out, x1, None, mods, post[1], pre[2], gate_i=5, weight=1.0, shift_i=6, scale_i=7)

    f2, _ = _ffn(h2, ffn_w_gate, ffn_w_up, ffn_w_down, (l, 1), rows_x)
    x3 = _residual(f2, x2, None, mods, post[2], None, gate_i=8, weight=MACARON_W)[0]
    return x3.reshape(bsz, n_lat, d)
```
